```python
import jax, jax.numpy as jnp
from jax import lax
import numpy as np

D_MODEL = 1024
BATCH = 16
SEQ = 2048
DEPTH = 2
DEC_BATCH = 4
DEC_SEQ = 8192
PAST_LEN = 128

FFN_DIM = 2816
SGU_DIM = 3 * D_MODEL
SGU_GROUPS = 8
SGU_CHUNK = 128
HGRN_DK = 128
HGRN_HEADS = D_MODEL // HGRN_DK
HGRN_DV = D_MODEL // HGRN_HEADS
HGRN_HK = HGRN_HEADS * HGRN_DK
HGRN_HV = HGRN_HEADS * HGRN_DV
HGRN_CHUNK = 64
N_SGU_LAYERS = (DEPTH + 1) // 2
N_HGRN_LAYERS = DEPTH // 2
EPS = 1e-6

kernel_name = 'hybrid_sgu_hgrn2_macaron_encoder'


def _rmsnorm(x, w):
    xf = x.astype(jnp.float32)
    y = xf * lax.rsqrt(jnp.mean(xf * xf, axis=-1, keepdims=True) + EPS)
    return (y * w.astype(jnp.float32)).astype(x.dtype)


def _layernorm(x, g, b):
    xf = x.astype(jnp.float32)
    mu = jnp.mean(xf, axis=-1, keepdims=True)
    xc = xf - mu
    y = xc * lax.rsqrt(jnp.mean(xc * xc, axis=-1, keepdims=True) + EPS)
    return (y * g.astype(jnp.float32) + b.astype(jnp.float32)).astype(x.dtype)


def _swiglu(h, wg, wu, wd):
    return (jax.nn.silu(h @ wg) * (h @ wu)) @ wd


def _spatial_gating(h, w_in, ln_g, ln_b, w_s, b_s, w_out):
    B, L, _ = h.shape
    z = jax.nn.gelu(h @ w_in, approximate=False)
    u, v = jnp.split(z, 2, axis=-1)
    v = _layernorm(v, ln_g, ln_b)
    v = v.reshape(B, L // SGU_CHUNK, SGU_CHUNK, SGU_GROUPS, SGU_DIM // SGU_GROUPS)
    s = jnp.einsum('gts,bnsge->bntge', w_s, v) + b_s.T[None, None, :, :, None]
    return (u * s.reshape(B, L, SGU_DIM)) @ w_out


def _gla_scan(q, k, v, g):
    B, L, H, DK = q.shape
    DV = v.shape[-1]
    n = L // HGRN_CHUNK

    def to_chunks(t):
        return t.reshape(B, n, HGRN_CHUNK, H, t.shape[-1]).transpose(1, 0, 3, 2, 4)

    mask = jnp.tril(jnp.ones((HGRN_CHUNK, HGRN_CHUNK), dtype=bool))[:, :, None]

    def step(S, inp):
        qc, kc, vc, gc = inp
        b = jnp.cumsum(gc, axis=2)
        diff = b[:, :, :, None, :] - b[:, :, None, :, :]
        decay = jnp.exp(jnp.where(mask, diff, -jnp.inf))
        a = jnp.einsum('bhtsk,bhsk->bhts', qc[:, :, :, None, :] * decay, kc)
        o = jnp.einsum('bhts,bhsv->bhtv', a, vc) + jnp.einsum('bhtk,bhkv->bhtv', qc * jnp.exp(b), S)
        b_last = b[:, :, -1:, :]
        S = jnp.exp(b_last)[:, :, 0, :, None] * S + jnp.einsum('bhsk,bhsv->bhkv', kc * jnp.exp(b_last - b), vc)
        return S, o

    S0 = jnp.zeros((B, H, DK, DV), jnp.float32)
    _, o = lax.scan(step, S0, (to_chunks(q), to_chunks(k), to_chunks(v), to_chunks(g)))
    return o.transpose(1, 0, 3, 2, 4).reshape(B, L, H, DV)


def _hgrn2(h, w_in, lb_f, lb_b, norm_w, w_out):
    B, L, _ = h.shape
    proj = (h @ w_in).astype(jnp.float32)
    q, ff, fb, i, g = jnp.split(proj, [HGRN_HK, 2 * HGRN_HK, 3 * HGRN_HK, 3 * HGRN_HK + HGRN_HV], axis=-1)

    def heads(t):
        return t.reshape(B, L, HGRN_HEADS, -1)

    def gates(logit, lb):
        f = lb + (1.0 - lb) * jax.nn.sigmoid(logit)
        return heads(1.0 - f), heads(jnp.log(f))

    q = heads(jax.nn.silu(q))
    i = heads(i)
    kf, gf = gates(ff, lb_f)
    kb, gb = gates(fb, lb_b)
    o_f = _gla_scan(q, kf, i, gf)
    o_b = jnp.flip(_gla_scan(jnp.flip(q, 1), jnp.flip(kb, 1), jnp.flip(i, 1), jnp.flip(gb, 1)), 1)
    o = o_f + o_b
    o = o * lax.rsqrt(jnp.mean(o * o, axis=-1, keepdims=True) + EPS)
    o = o.reshape(B, L, HGRN_HV) * norm_w.astype(jnp.float32) * jax.nn.silu(g)
    return o.astype(h.dtype) @ w_out


def _trunk(x, norm_w, ffn_gate, ffn_up, ffn_down, sgu_w_in, sgu_ln_g, sgu_ln_b, sgu_w_s, sgu_b_s,
           sgu_w_out, hgrn_w_in, lb, hgrn_norm_w, hgrn_w_out, final_norm):
    for layer in range(DEPTH):
        x = x + 0.5 * _swiglu(_rmsnorm(x, norm_w[layer, 0]), ffn_gate[layer, 0], ffn_up[layer, 0], ffn_down[layer, 0])
        h = _rmsnorm(x, norm_w[layer, 1])
        j = layer // 2
        if layer % 2 == 0:
            x = x + _spatial_gating(h, sgu_w_in[j], sgu_ln_g[j], sgu_ln_b[j], sgu_w_s[j], sgu_b_s[j], sgu_w_out[j])
        else:
            x = x + _hgrn2(h, hgrn_w_in[j], lb[0, layer], lb[1, layer], hgrn_norm_w[j], hgrn_w_out[j])
        x = x + 0.5 * _swiglu(_rmsnorm(x, norm_w[layer, 2]), ffn_gate[layer, 1], ffn_up[layer, 1], ffn_down[layer, 1])
    return _rmsnorm(x, final_norm)


def setup_inputs(seed: int = 0) -> dict:
    key = jax.random.key(seed)
    ks = jax.random.split(key, 20)
    nrm = jax.random.normal
    D, F = D_MODEL, FFN_DIM
    return {
        'x_prompt': nrm(ks[0], (BATCH, SEQ, D), jnp.float32),
        'x_sample': nrm(ks[1], (DEC_BATCH, DEC_SEQ, D), jnp.float32),
        'norm_w': 1.0 + 0.02 * nrm(ks[2], (DEPTH, 3, D), jnp.float32),
        'ffn_gate': nrm(ks[3], (DEPTH, 2, D, F), jnp.float32) * D ** -0.5,
        'ffn_up': nrm(ks[4], (DEPTH, 2, D, F), jnp.float32) * D ** -0.5,
        'ffn_down': nrm(ks[5], (DEPTH, 2, F, D), jnp.float32) * F ** -0.5,
        'sgu_w_in': nrm(ks[6], (N_SGU_LAYERS, D, 2 * SGU_DIM), jnp.float32) * D ** -0.5,
        'sgu_ln_g': 1.0 + 0.02 * nrm(ks[7], (N_SGU_LAYERS, SGU_DIM), jnp.float32),
        'sgu_ln_b': 0.02 * nrm(ks[8], (N_SGU_LAYERS, SGU_DIM), jnp.float32),
        'sgu_w_s': nrm(ks[9], (N_SGU_LAYERS, SGU_GROUPS, SGU_CHUNK, SGU_CHUNK), jnp.float32) * SGU_CHUNK ** -0.5,
        'sgu_b_s': 1.0 + 0.02 * nrm(ks[10], (N_SGU_LAYERS, SGU_GROUPS, SGU_CHUNK), jnp.float32),
        'sgu_w_out': nrm(ks[11], (N_SGU_LAYERS, SGU_DIM, D), jnp.float32) * SGU_DIM ** -0.5,
        'hgrn_w_in': nrm(ks[12], (N_HGRN_LAYERS, D, 3 * HGRN_HK + 2 * HGRN_HV), jnp.float32) * D ** -0.5,
        'hgrn_lb_raw': 0.1 * nrm(ks[13], (2, DEPTH, HGRN_HK), jnp.float32),
        'hgrn_norm_w': 1.0 + 0.02 * nrm(ks[14], (N_HGRN_LAYERS, HGRN_HV), jnp.float32),
        'hgrn_w_out': nrm(ks[15], (N_HGRN_LAYERS, HGRN_HV, D), jnp.float32) * HGRN_HV ** -0.5,
        'final_norm': 1.0 + 0.02 * nrm(ks[16], (D,), jnp.float32),
    }


def reference(x_prompt, x_sample, norm_w, ffn_gate, ffn_up, ffn_down, sgu_w_in, sgu_ln_g, sgu_ln_b,
              sgu_w_s, sgu_b_s, sgu_w_out, hgrn_w_in, hgrn_lb_raw, hgrn_norm_w, hgrn_w_out, final_norm):
    p = jax.nn.softmax(hgrn_lb_raw.astype(jnp.float32), axis=1)
    lb = jnp.cumsum(p, axis=1) - p[:, :1]
    y_prompt = _trunk(x_prompt, norm_w, ffn_gate, ffn_up, ffn_down, sgu_w_in, sgu_ln_g, sgu_ln_b, sgu_w_s,
                      sgu_b_s, sgu_w_out, hgrn_w_in, lb, hgrn_norm_w, hgrn_w_out, final_norm)
    y_sample = _trunk(x_sample, norm_w, ffn_gate, ffn_up, ffn_down, sgu_w_in, sgu_ln_g, sgu_ln_b, sgu_w_s,
                      sgu_b_s, sgu_w_out, hgrn_w_in, lb, hgrn_norm_w, hgrn_w_out, final_norm)
    return (y_prompt, y_sample)
```

```python
import functools

import jax
import jax.numpy as jnp
from jax import lax
from jax.experimental import pallas as pl
from jax.experimental.pallas import tpu as pltpu

D_MODEL = 1024
FFN_DIM = 2816
SGU_DIM = 3 * D_MODEL
SGU_GROUPS = 8
SGU_GROUP_DIM = SGU_DIM // SGU_GROUPS
SGU_CHUNK = 128
HEAD_DIM = 128
HEADS = D_MODEL // HEAD_DIM
EPS = 1e-6

SCAN_CHUNK = 64
SUB_BLOCK = 16
SUB_REF = SUB_BLOCK // 2

FFN_ROWS = 512
SGU_ROWS = 256
HGRN_ROWS = 256
SCAN_ROWS = 256

V7X_VMEM_LIMIT_BYTES = 56 * 1024 * 1024

_F32 = jnp.float32
_BF16 = jnp.bfloat16
_NT = (((1,), (1,)), ((), ()))
_TN = (((0,), (0,)), ((), ()))


def _params(*semantics):
    return pltpu.CompilerParams(dimension_semantics=semantics, vmem_limit_bytes=V7X_VMEM_LIMIT_BYTES)


def _row_spec(rows, cols):
    return pl.BlockSpec((rows, cols), lambda i: (i, 0))


def _resident_spec(shape):
    zeros = (0,) * len(shape)
    return pl.BlockSpec(shape, lambda *_: zeros, pipeline_mode=pl.Buffered(1))


def _rmsnorm(x, w):
    return x * lax.rsqrt(jnp.mean(x * x, axis=-1, keepdims=True) + EPS) * w


def _sigmoid(x):
    return 1.0 / (1.0 + jnp.exp(-x))


def _dot(a, b):
    return jnp.dot(a, b, preferred_element_type=_F32)


def _ffn_kernel(x_ref, nw_ref, wg_ref, wu_ref, wd_ref, fw_ref, o_ref, *, final_norm):
    x = x_ref[...]
    h = _rmsnorm(x, nw_ref[...]).astype(_BF16)
    gate = _dot(h, wg_ref[...])
    up = _dot(h, wu_ref[...])
    act = (gate * _sigmoid(gate) * up).astype(_BF16)
    y = x + 0.5 * _dot(act, wd_ref[...])
    if final_norm:
        y = _rmsnorm(y, fw_ref[...])
    o_ref[...] = y


def _ffn(x, nw, wg, wu, wd, fw, *, final_norm):
    rows = x.shape[0]
    return pl.pallas_call(
        functools.partial(_ffn_kernel, final_norm=final_norm),
        out_shape=jax.ShapeDtypeStruct(x.shape, _F32),
        grid=(rows // FFN_ROWS,),
        in_specs=[
            _row_spec(FFN_ROWS, D_MODEL),
            _resident_spec((1, D_MODEL)),
            _resident_spec((D_MODEL, FFN_DIM)),
            _resident_spec((D_MODEL, FFN_DIM)),
            _resident_spec((FFN_DIM, D_MODEL)),
            _resident_spec((1, D_MODEL)),
        ],
        out_specs=_row_spec(FFN_ROWS, D_MODEL),
        compiler_params=_params("parallel"),
        name="ffn",
    )(x, nw, wg, wu, wd, fw)


def _sgu_kernel(x_ref, nw_ref, win_ref, lng_ref, lnb_ref, ws_ref, bs_ref, wout_ref, o_ref):
    x = x_ref[...]
    h = _rmsnorm(x, nw_ref[...]).astype(_BF16)
    z = _dot(h, win_ref[...])
    z = 0.5 * z * (1.0 + lax.erf(z * (2.0 ** -0.5)))
    u = z[:, :SGU_DIM]
    v = z[:, SGU_DIM:]
    mu = jnp.mean(v, axis=-1, keepdims=True)
    vc = v - mu
    v = vc * lax.rsqrt(jnp.mean(vc * vc, axis=-1, keepdims=True) + EPS) * lng_ref[...] + lnb_ref[...]
    v = v.astype(_BF16)
    gated = []
    for n in range(SGU_ROWS // SGU_CHUNK):
        rows = slice(n * SGU_CHUNK, (n + 1) * SGU_CHUNK)
        parts = []
        for g in range(SGU_GROUPS):
            cols = slice(g * SGU_GROUP_DIM, (g + 1) * SGU_GROUP_DIM)
            parts.append(_dot(ws_ref[g], v[rows, cols]))
        s = jnp.concatenate(parts, axis=1) + bs_ref[...]
        gated.append((u[rows] * s).astype(_BF16))
    gated = jnp.concatenate(gated, axis=0)
    o_ref[...] = x + _dot(gated, wout_ref[...])


def _sgu(x, nw, win, lng, lnb, ws, bs, wout):
    rows = x.shape[0]
    return pl.pallas_call(
        _sgu_kernel,
        out_shape=jax.ShapeDtypeStruct(x.shape, _F32),
        grid=(rows // SGU_ROWS,),
        in_specs=[
            _row_spec(SGU_ROWS, D_MODEL),
            _resident_spec((1, D_MODEL)),
            _resident_spec((D_MODEL, 2 * SGU_DIM)),
            _resident_spec((1, SGU_DIM)),
            _resident_spec((1, SGU_DIM)),
            _resident_spec((SGU_GROUPS, SGU_CHUNK, SGU_CHUNK)),
            _resident_spec((SGU_CHUNK, SGU_DIM)),
            _resident_spec((SGU_DIM, D_MODEL)),
        ],
        out_specs=_row_spec(SGU_ROWS, D_MODEL),
        compiler_params=_params("parallel"),
        name="sgu",
    )(x, nw, win, lng, lnb, ws, bs, wout)


def _lower_bound(raw, layer):
    e = jnp.exp(raw - jnp.max(raw, axis=0, keepdims=True))
    p = e / jnp.sum(e, axis=0, keepdims=True)
    lb = jnp.zeros_like(p[0:1])
    for j in range(1, layer + 1):
        lb = lb + p[j:j + 1]
    return lb


def _chunk_cumsum(ones_bf16, x):
    hi = x.astype(_BF16)
    r1 = x - hi.astype(_F32)
    mid = r1.astype(_BF16)
    lo = (r1 - mid.astype(_F32)).astype(_BF16)
    return _dot(ones_bf16, hi) + _dot(ones_bf16, mid) + _dot(ones_bf16, lo)


def _hgrn_pre_kernel(x_ref, nw_ref, win_ref, lbf_ref, lbb_ref,
                     q_ref, kf_ref, kb_ref, v_ref, gs_ref, bf_ref, bb_ref, *, layer):
    x = x_ref[...]
    h = _rmsnorm(x, nw_ref[...]).astype(_BF16)
    p = _dot(h, win_ref[...])
    q = p[:, 0 * D_MODEL:1 * D_MODEL]
    q_ref[...] = (q * _sigmoid(q)).astype(_BF16)
    v_ref[...] = p[:, 3 * D_MODEL:4 * D_MODEL].astype(_BF16)
    g = p[:, 4 * D_MODEL:5 * D_MODEL]
    gs_ref[...] = (g * _sigmoid(g)).astype(_BF16)

    row = lax.broadcasted_iota(jnp.int32, (HGRN_ROWS, HGRN_ROWS), 0)
    col = lax.broadcasted_iota(jnp.int32, (HGRN_ROWS, HGRN_ROWS), 1)
    same_chunk = (row // SCAN_CHUNK) == (col // SCAN_CHUNK)
    prefix = jnp.where(same_chunk & (col <= row), 1.0, 0.0).astype(_BF16)
    suffix = jnp.where(same_chunk & (col >= row), 1.0, 0.0).astype(_BF16)

    lb = _lower_bound(lbf_ref[...], layer)
    f = lb + (1.0 - lb) * _sigmoid(p[:, 1 * D_MODEL:2 * D_MODEL])
    kf_ref[...] = (1.0 - f).astype(_BF16)
    bf_ref[...] = _chunk_cumsum(prefix, jnp.log(f))

    lb = _lower_bound(lbb_ref[...], layer)
    f = lb + (1.0 - lb) * _sigmoid(p[:, 2 * D_MODEL:3 * D_MODEL])
    kb_ref[...] = (1.0 - f).astype(_BF16)
    bb_ref[...] = _chunk_cumsum(suffix, jnp.log(f))


def _hgrn_pre(x, nw, win, lbf_raw, lbb_raw, *, layer):
    rows = x.shape[0]
    depth = lbf_raw.shape[0]
    half = jax.ShapeDtypeStruct(x.shape, _BF16)
    full = jax.ShapeDtypeStruct(x.shape, _F32)
    spec = _row_spec(HGRN_ROWS, D_MODEL)
    return pl.pallas_call(
        functools.partial(_hgrn_pre_kernel, layer=layer),
        out_shape=(half, half, half, half, half, full, full),
        grid=(rows // HGRN_ROWS,),
        in_specs=[
            spec,
            _resident_spec((1, D_MODEL)),
            _resident_spec((D_MODEL, 5 * D_MODEL)),
            _resident_spec((depth, D_MODEL)),
            _resident_spec((depth, D_MODEL)),
        ],
        out_specs=(spec,) * 7,
        compiler_params=_params("parallel"),
        name="hgrn_pre",
    )(x, nw, win, lbf_raw, lbb_raw)


def _scan_kernel(*refs, reverse, has_prev):
    if has_prev:
        q_ref, k_ref, v_ref, b_ref, prev_ref, o_ref, st_ref = refs
    else:
        q_ref, k_ref, v_ref, b_ref, o_ref, st_ref = refs
        prev_ref = None

    @pl.when(pl.program_id(1) == 0)
    def _():
        st_ref[...] = jnp.zeros_like(st_ref)

    nchunks = SCAN_ROWS // SCAN_CHUNK
    nsub = SCAN_CHUNK // SUB_BLOCK
    exit_row = 0 if reverse else SCAN_CHUNK - 1

    def chunk_body(ci, carry):
        c = (nchunks - 1 - ci) if reverse else ci
        r0 = pl.multiple_of(c * SCAN_CHUNK, SCAN_CHUNK)
        for h in range(HEADS):
            hs = slice(h * HEAD_DIM, (h + 1) * HEAD_DIM)
            b = b_ref[pl.ds(r0, SCAN_CHUNK), hs]
            q = q_ref[pl.ds(r0, SCAN_CHUNK), hs].astype(_F32)
            k = k_ref[pl.ds(r0, SCAN_CHUNK), hs].astype(_F32)
            v = v_ref[pl.ds(r0, SCAN_CHUNK), hs]
            b_exit = b[exit_row:exit_row + 1]
            state = st_ref[h]
            q_in = (q * jnp.exp(b)).astype(_BF16)
            o_inter = lax.dot_general(q_in, state.astype(_BF16), _NT, preferred_element_type=_F32)
            for blk in range(nsub):
                rs = slice(blk * SUB_BLOCK, (blk + 1) * SUB_BLOCK)
                cs = slice(blk * SUB_BLOCK, SCAN_CHUNK) if reverse else slice(0, (blk + 1) * SUB_BLOCK)
                ncols = cs.stop - cs.start
                b_mid = b[blk * SUB_BLOCK + SUB_REF:blk * SUB_BLOCK + SUB_REF + 1]
                q_blk = (q[rs] * jnp.exp(b[rs] - b_mid)).astype(_BF16)
                k_blk = (k[cs] * jnp.exp(b_mid - b[cs])).astype(_BF16)
                a = lax.dot_general(q_blk, k_blk, _NT, preferred_element_type=_F32)
                t_idx = lax.broadcasted_iota(jnp.int32, (SUB_BLOCK, ncols), 0) + blk * SUB_BLOCK
                s_idx = lax.broadcasted_iota(jnp.int32, (SUB_BLOCK, ncols), 1) + cs.start
                seen = (s_idx >= t_idx) if reverse else (s_idx <= t_idx)
                a = jnp.where(seen, a, 0.0).astype(_BF16)
                o_blk = o_inter[rs] + _dot(a, v[cs])
                rows = pl.ds(r0 + blk * SUB_BLOCK, SUB_BLOCK)
                if has_prev:
                    o_blk = o_blk + prev_ref[rows, hs]
                o_ref[rows, hs] = o_blk
            k_out = (k * jnp.exp(b_exit - b)).astype(_BF16)
            update = lax.dot_general(v, k_out, _TN, preferred_element_type=_F32)
            st_ref[h] = state * jnp.exp(b_exit) + update
        return carry

    lax.fori_loop(0, nchunks, chunk_body, 0)


def _scan(q, k, v, b, prev, *, batch, reverse):
    rows = q.shape[0]
    nblocks = rows // batch // SCAN_ROWS
    if reverse:
        index_map = lambda bi, n: (bi * nblocks + nblocks - 1 - n, 0)
    else:
        index_map = lambda bi, n: (bi * nblocks + n, 0)
    spec = pl.BlockSpec((SCAN_ROWS, D_MODEL), index_map)
    operands = (q, k, v, b) + (() if prev is None else (prev,))
    return pl.pallas_call(
        functools.partial(_scan_kernel, reverse=reverse, has_prev=prev is not None),
        out_shape=jax.ShapeDtypeStruct(q.shape, _F32),
        grid=(batch, nblocks),
        in_specs=[spec] * len(operands),
        out_specs=spec,
        scratch_shapes=[pltpu.VMEM((HEADS, HEAD_DIM, HEAD_DIM), _F32)],
        compiler_params=_params("parallel", "arbitrary"),
        name="scan_bwd" if reverse else "scan_fwd",
    )(*operands)


def _hgrn_post_kernel(x_ref, o_ref, gs_ref, nw_ref, wout_ref, y_ref):
    o = o_ref[...]
    parts = []
    for h in range(HEADS):
        oh = o[:, h * HEAD_DIM:(h + 1) * HEAD_DIM]
        parts.append(oh * lax.rsqrt(jnp.mean(oh * oh, axis=-1, keepdims=True) + EPS))
    o = jnp.concatenate(parts, axis=1) * nw_ref[...] * gs_ref[...].astype(_F32)
    y_ref[...] = x_ref[...] + _dot(o.astype(_BF16), wout_ref[...])


def _hgrn_post(x, o, gs, nw, wout):
    rows = x.shape[0]
    spec = _row_spec(HGRN_ROWS, D_MODEL)
    return pl.pallas_call(
        _hgrn_post_kernel,
        out_shape=jax.ShapeDtypeStruct(x.shape, _F32),
        grid=(rows // HGRN_ROWS,),
        in_specs=[spec, spec, spec, _resident_spec((1, D_MODEL)), _resident_spec((D_MODEL, D_MODEL))],
        out_specs=spec,
        compiler_params=_params("parallel"),
        name="hgrn_post",
    )(x, o, gs, nw, wout)


def _trunk(x, w):
    batch, seq, _ = x.shape
    x = x.reshape(batch * seq, D_MODEL)
    depth = w["norm_w"].shape[0]
    for layer in range(depth):
        last = layer == depth - 1
        j = layer // 2
        x = _ffn(x, w["norm_w"][layer, 0][None], w["ffn_gate"][layer, 0], w["ffn_up"][layer, 0],
                 w["ffn_down"][layer, 0], w["final_norm"][None], final_norm=False)
        if layer % 2 == 0:
            x = _sgu(x, w["norm_w"][layer, 1][None], w["sgu_w_in"][j], w["sgu_ln_g"][j][None],
                     w["sgu_ln_b"][j][None], w["sgu_w_s"][j], w["sgu_b_s"][j], w["sgu_w_out"][j])
        else:
            q, kf, kb, v, gs, bf, bb = _hgrn_pre(x, w["norm_w"][layer, 1][None], w["hgrn_w_in"][j],
                                                 w["hgrn_lb_raw"][0], w["hgrn_lb_raw"][1], layer=layer)
            o = _scan(q, kf, v, bf, None, batch=batch, reverse=False)
            o = _scan(q, kb, v, bb, o, batch=batch, reverse=True)
            x = _hgrn_post(x, o, gs, w["hgrn_norm_w"][j][None], w["hgrn_w_out"][j])
        x = _ffn(x, w["norm_w"][layer, 2][None], w["ffn_gate"][layer, 1], w["ffn_up"][layer, 1],
                 w["ffn_down"][layer, 1], w["final_norm"][None], final_norm=last)
    return x.reshape(batch, seq, D_MODEL)


def kernel(x_prompt, x_sample, norm_w, ffn_gate, ffn_up, ffn_down, sgu_w_in, sgu_ln_g, sgu_ln_b, sgu_w_s,
           sgu_b_s, sgu_w_out, hgrn_w_in, hgrn_lb_raw, hgrn_norm_w, hgrn_w_out, final_norm):
    w = {
        "norm_w": norm_w,
        "ffn_gate": ffn_gate.astype(_BF16),
        "ffn_up": ffn_up.astype(_BF16),
        "ffn_down": ffn_down.astype(_BF16),
        "sgu_w_in": sgu_w_in.astype(_BF16),
        "sgu_ln_g": sgu_ln_g,
        "sgu_ln_b": sgu_ln_b,
        "sgu_w_s": sgu_w_s.astype(_BF16),
        "sgu_b_s": jnp.repeat(jnp.swapaxes(sgu_b_s, 1, 2), SGU_GROUP_DIM, axis=2),
        "sgu_w_out": sgu_w_out.astype(_BF16),
        "hgrn_w_in": hgrn_w_in.astype(_BF16),
        "hgrn_lb_raw": hgrn_lb_raw,
        "hgrn_norm_w": hgrn_norm_w,
        "hgrn_w_out": hgrn_w_out.astype(_BF16),
        "final_norm": final_norm,
    }
    return _trunk(x_prompt, w), _trunk(x_sample, w)
```

```python
import functools

import jax
import jax.numpy as jnp
from jax import lax
from jax.experimental import pallas as pl
from jax.experimental.pallas import tpu as pltpu

D_MODEL = 1024
FFN_DIM = 2816
SGU_DIM = 3 * D_MODEL
SGU_GROUPS = 8
SGU_GROUP_DIM = SGU_DIM // SGU_GROUPS
SGU_CHUNK = 128
HEAD_DIM = 128
HEADS = D_MODEL // HEAD_DIM
EPS = 1e-6

SCAN_CHUNK = 64
SUB_BLOCK = 16
SUB_REF = SUB_BLOCK // 2
SCAN_SEQS = 2

FFN_ROWS = 512
SGU_ROWS = 512
SGU_PART = 256
HGRN_ROWS = 256

V7X_VMEM_LIMIT_BYTES = 56 * 1024 * 1024

_F32 = jnp.float32
_BF16 = jnp.bfloat16
_NT = (((1,), (1,)), ((), ()))
_TN = (((0,), (0,)), ((), ()))


def _params(*semantics):
    return pltpu.CompilerParams(dimension_semantics=semantics, vmem_limit_bytes=V7X_VMEM_LIMIT_BYTES)


def _row_spec(rows, cols):
    return pl.BlockSpec((rows, cols), lambda i: (i, 0))


def _resident_spec(shape):
    zeros = (0,) * len(shape)
    return pl.BlockSpec(shape, lambda *_: zeros, pipeline_mode=pl.Buffered(1))


def _rmsnorm(x, w):
    return x * lax.rsqrt(jnp.mean(x * x, axis=-1, keepdims=True) + EPS) * w


def _sigmoid(x):
    return 1.0 / (1.0 + jnp.exp(-x))


def _dot(a, b):
    return jnp.dot(a, b, preferred_element_type=_F32)


def _ffn_kernel(x_ref, nw_ref, wg_ref, wu_ref, wd_ref, fw_ref, o_ref, *, final_norm):
    x = x_ref[...]
    h = _rmsnorm(x, nw_ref[...]).astype(_BF16)
    gate = _dot(h, wg_ref[...])
    up = _dot(h, wu_ref[...])
    act = (gate * _sigmoid(gate) * up).astype(_BF16)
    y = x + 0.5 * _dot(act, wd_ref[...])
    if final_norm:
        y = _rmsnorm(y, fw_ref[...])
    o_ref[...] = y


def _ffn(x, nw, wg, wu, wd, fw, *, final_norm):
    rows = x.shape[0]
    return pl.pallas_call(
        functools.partial(_ffn_kernel, final_norm=final_norm),
        out_shape=jax.ShapeDtypeStruct(x.shape, _F32),
        grid=(rows // FFN_ROWS,),
        in_specs=[
            _row_spec(FFN_ROWS, D_MODEL),
            _resident_spec((1, D_MODEL)),
            _resident_spec((D_MODEL, FFN_DIM)),
            _resident_spec((D_MODEL, FFN_DIM)),
            _resident_spec((FFN_DIM, D_MODEL)),
            _resident_spec((1, D_MODEL)),
        ],
        out_specs=_row_spec(FFN_ROWS, D_MODEL),
        compiler_params=_params("parallel"),
        name="ffn",
    )(x, nw, wg, wu, wd, fw)


def _sgu_kernel(x_ref, nw_ref, win_ref, lng_ref, lnb_ref, ws_ref, bs_ref, wout_ref, o_ref):
    def project(n):
        rows = slice(n * SGU_PART, (n + 1) * SGU_PART)
        h = _rmsnorm(x_ref[rows, :], nw_ref[...]).astype(_BF16)
        z = _dot(h, win_ref[...])
        z = 0.5 * z * (1.0 + lax.erf(z * (2.0 ** -0.5)))
        u = z[:, :SGU_DIM]
        v = z[:, SGU_DIM:]
        mu = jnp.mean(v, axis=-1, keepdims=True)
        vc = v - mu
        v = vc * lax.rsqrt(jnp.mean(vc * vc, axis=-1, keepdims=True) + EPS) * lng_ref[...] + lnb_ref[...]
        return u.astype(_BF16), v.astype(_BF16)

    def mix(n, u, v):
        gated = []
        for c in range(SGU_PART // SGU_CHUNK):
            chunk = slice(c * SGU_CHUNK, (c + 1) * SGU_CHUNK)
            parts = []
            for g in range(SGU_GROUPS):
                cols = slice(g * SGU_GROUP_DIM, (g + 1) * SGU_GROUP_DIM)
                parts.append(_dot(ws_ref[g], v[chunk, cols]))
            s = jnp.concatenate(parts, axis=1) + bs_ref[...]
            gated.append((u[chunk].astype(_F32) * s).astype(_BF16))
        rows = slice(n * SGU_PART, (n + 1) * SGU_PART)
        o_ref[rows, :] = x_ref[rows, :] + _dot(jnp.concatenate(gated, axis=0), wout_ref[...])

    nparts = SGU_ROWS // SGU_PART
    pending = project(0)
    for n in range(1, nparts):
        following = project(n)
        mix(n - 1, *pending)
        pending = following
    mix(nparts - 1, *pending)


def _sgu(x, nw, win, lng, lnb, ws, bs, wout):
    rows = x.shape[0]
    return pl.pallas_call(
        _sgu_kernel,
        out_shape=jax.ShapeDtypeStruct(x.shape, _F32),
        grid=(rows // SGU_ROWS,),
        in_specs=[
            _row_spec(SGU_ROWS, D_MODEL),
            _resident_spec((1, D_MODEL)),
            _resident_spec((D_MODEL, 2 * SGU_DIM)),
            _resident_spec((1, SGU_DIM)),
            _resident_spec((1, SGU_DIM)),
            _resident_spec((SGU_GROUPS, SGU_CHUNK, SGU_CHUNK)),
            _resident_spec((SGU_CHUNK, SGU_DIM)),
            _resident_spec((SGU_DIM, D_MODEL)),
        ],
        out_specs=_row_spec(SGU_ROWS, D_MODEL),
        compiler_params=_params("parallel"),
        name="sgu",
    )(x, nw, win, lng, lnb, ws, bs, wout)


def _lower_bound(raw, layer):
    e = jnp.exp(raw - jnp.max(raw, axis=0, keepdims=True))
    p = e / jnp.sum(e, axis=0, keepdims=True)
    lb = jnp.zeros_like(p[0:1])
    for j in range(1, layer + 1):
        lb = lb + p[j:j + 1]
    return lb


def _chunk_cumsum(ones_bf16, x):
    hi = x.astype(_BF16)
    r1 = x - hi.astype(_F32)
    mid = r1.astype(_BF16)
    lo = (r1 - mid.astype(_F32)).astype(_BF16)
    return _dot(ones_bf16, hi) + _dot(ones_bf16, mid) + _dot(ones_bf16, lo)


def _hgrn_pre_kernel(x_ref, nw_ref, win_ref, lbf_ref, lbb_ref,
                     q_ref, kf_ref, kb_ref, v_ref, gs_ref, bf_ref, bb_ref, *, layer):
    x = x_ref[...]
    h = _rmsnorm(x, nw_ref[...]).astype(_BF16)
    p = _dot(h, win_ref[...])
    q = p[:, 0 * D_MODEL:1 * D_MODEL]
    q_ref[...] = (q * _sigmoid(q)).astype(_BF16)
    v_ref[...] = p[:, 3 * D_MODEL:4 * D_MODEL].astype(_BF16)
    g = p[:, 4 * D_MODEL:5 * D_MODEL]
    gs_ref[...] = (g * _sigmoid(g)).astype(_BF16)

    row = lax.broadcasted_iota(jnp.int32, (HGRN_ROWS, HGRN_ROWS), 0)
    col = lax.broadcasted_iota(jnp.int32, (HGRN_ROWS, HGRN_ROWS), 1)
    same_chunk = (row // SCAN_CHUNK) == (col // SCAN_CHUNK)
    prefix = jnp.where(same_chunk & (col <= row), 1.0, 0.0).astype(_BF16)
    suffix = jnp.where(same_chunk & (col >= row), 1.0, 0.0).astype(_BF16)

    lb = _lower_bound(lbf_ref[...], layer)
    f = lb + (1.0 - lb) * _sigmoid(p[:, 1 * D_MODEL:2 * D_MODEL])
    kf_ref[...] = (1.0 - f).astype(_BF16)
    bf_ref[...] = _chunk_cumsum(prefix, jnp.log2(f))

    lb = _lower_bound(lbb_ref[...], layer)
    f = lb + (1.0 - lb) * _sigmoid(p[:, 2 * D_MODEL:3 * D_MODEL])
    kb_ref[...] = (1.0 - f).astype(_BF16)
    bb_ref[...] = _chunk_cumsum(suffix, jnp.log2(f))


def _hgrn_pre(x, nw, win, lbf_raw, lbb_raw, *, layer):
    rows = x.shape[0]
    depth = lbf_raw.shape[0]
    half = jax.ShapeDtypeStruct(x.shape, _BF16)
    full = jax.ShapeDtypeStruct(x.shape, _F32)
    spec = _row_spec(HGRN_ROWS, D_MODEL)
    return pl.pallas_call(
        functools.partial(_hgrn_pre_kernel, layer=layer),
        out_shape=(half, half, half, half, half, full, full),
        grid=(rows // HGRN_ROWS,),
        in_specs=[
            spec,
            _resident_spec((1, D_MODEL)),
            _resident_spec((D_MODEL, 5 * D_MODEL)),
            _resident_spec((depth, D_MODEL)),
            _resident_spec((depth, D_MODEL)),
        ],
        out_specs=(spec,) * 7,
        compiler_params=_params("parallel"),
        name="hgrn_pre",
    )(x, nw, win, lbf_raw, lbb_raw)


def _scan_scores(q_ref, k_ref, v_ref, b_ref, st_ref, seq, head, slot, reverse):
    nsub = SCAN_CHUNK // SUB_BLOCK
    exit_row = 0 if reverse else SCAN_CHUNK - 1
    hs = slice(head * HEAD_DIM, (head + 1) * HEAD_DIM)
    b = b_ref[seq, :, hs]
    q = q_ref[seq, :, hs]
    k = k_ref[seq, :, hs]
    v = v_ref[seq, :, hs]
    b_exit = b[exit_row:exit_row + 1]
    state = st_ref[slot]
    q_in = q * jnp.exp2(b).astype(_BF16)
    o_inter = lax.dot_general(q_in, state.astype(_BF16), _NT, preferred_element_type=_F32)
    zero_block = jnp.zeros((SUB_BLOCK, HEAD_DIM), _BF16)
    q_blocks, k_blocks = [], []
    for blk in range(nsub):
        rs = slice(blk * SUB_BLOCK, (blk + 1) * SUB_BLOCK)
        cs = slice(blk * SUB_BLOCK, SCAN_CHUNK) if reverse else slice(0, (blk + 1) * SUB_BLOCK)
        b_mid = b[blk * SUB_BLOCK + SUB_REF:blk * SUB_BLOCK + SUB_REF + 1]
        q_row = [zero_block] * nsub
        q_row[blk] = q[rs] * jnp.exp2(b[rs] - b_mid).astype(_BF16)
        q_blocks.append(jnp.concatenate(q_row, axis=1))
        k_col = [zero_block] * nsub
        k_col[cs.start // SUB_BLOCK:cs.stop // SUB_BLOCK] = [k[cs] * jnp.exp2(b_mid - b[cs]).astype(_BF16)]
        k_blocks.append(jnp.concatenate(k_col, axis=0))
    scores = lax.dot_general(jnp.concatenate(q_blocks, axis=0), jnp.concatenate(k_blocks, axis=1), _NT,
                             preferred_element_type=_F32)
    k_out = k * jnp.exp2(b_exit - b).astype(_BF16)
    update = lax.dot_general(v, k_out, _TN, preferred_element_type=_F32)
    st_ref[slot] = state * jnp.exp2(b_exit) + update
    return scores, o_inter, v


def _scan_kernel(qf_ref, kf_ref, vf_ref, bf_ref, qb_ref, kb_ref, vb_ref, bb_ref, of_ref, ob_ref, st_ref):
    @pl.when(pl.program_id(1) == 0)
    def _():
        st_ref[...] = jnp.zeros_like(st_ref)

    t_idx = lax.broadcasted_iota(jnp.int32, (SCAN_CHUNK, SCAN_CHUNK), 0)
    s_idx = lax.broadcasted_iota(jnp.int32, (SCAN_CHUNK, SCAN_CHUNK), 1)
    streams = []
    for seq in range(SCAN_SEQS):
        streams.append((seq, False, (qf_ref, kf_ref, vf_ref, bf_ref), of_ref, s_idx <= t_idx))
        streams.append((seq, True, (qb_ref, kb_ref, vb_ref, bb_ref), ob_ref, s_idx >= t_idx))

    def stage1(index):
        seq, reverse, refs, _, _ = streams[index]
        return [_scan_scores(*refs, st_ref, seq, h, index * HEADS + h, reverse) for h in range(HEADS)]

    def stage2(index, staged):
        seq, _, _, o_ref, seen = streams[index]
        for h, (scores, o_inter, v) in enumerate(staged):
            o_ref[seq, :, h * HEAD_DIM:(h + 1) * HEAD_DIM] = (
                o_inter + _dot(jnp.where(seen, scores, 0.0).astype(_BF16), v))

    staged = stage1(0)
    for index in range(1, len(streams)):
        following = stage1(index)
        stage2(index - 1, staged)
        staged = following
    stage2(len(streams) - 1, staged)


def _scan(q, kf, kb, v, bf, bb, *, batch):
    seq = q.shape[0] // batch
    nblocks = seq // SCAN_CHUNK
    shape3 = (batch, seq, D_MODEL)
    block = (SCAN_SEQS, SCAN_CHUNK, D_MODEL)
    fwd = pl.BlockSpec(block, lambda i, n: (i, n, 0))
    bwd = pl.BlockSpec(block, lambda i, n: (i, nblocks - 1 - n, 0))
    q, kf, kb, v, bf, bb = (a.reshape(shape3) for a in (q, kf, kb, v, bf, bb))
    out = jax.ShapeDtypeStruct(shape3, _F32)
    o_f, o_b = pl.pallas_call(
        _scan_kernel,
        out_shape=(out, out),
        grid=(batch // SCAN_SEQS, nblocks),
        in_specs=[fwd] * 4 + [bwd] * 4,
        out_specs=(fwd, bwd),
        scratch_shapes=[pltpu.VMEM((2 * SCAN_SEQS * HEADS, HEAD_DIM, HEAD_DIM), _F32)],
        compiler_params=_params("parallel", "arbitrary"),
        name="scan",
    )(q, kf, v, bf, q, kb, v, bb)
    return o_f.reshape(batch * seq, D_MODEL), o_b.reshape(batch * seq, D_MODEL)


def _hgrn_post_kernel(x_ref, of_ref, ob_ref, gs_ref, nw_ref, wout_ref, y_ref):
    o = of_ref[...] + ob_ref[...]
    parts = []
    for h in range(HEADS):
        oh = o[:, h * HEAD_DIM:(h + 1) * HEAD_DIM]
        parts.append(oh * lax.rsqrt(jnp.mean(oh * oh, axis=-1, keepdims=True) + EPS))
    o = jnp.concatenate(parts, axis=1) * nw_ref[...] * gs_ref[...].astype(_F32)
    y_ref[...] = x_ref[...] + _dot(o.astype(_BF16), wout_ref[...])


def _hgrn_post(x, o_f, o_b, gs, nw, wout):
    rows = x.shape[0]
    spec = _row_spec(HGRN_ROWS, D_MODEL)
    return pl.pallas_call(
        _hgrn_post_kernel,
        out_shape=jax.ShapeDtypeStruct(x.shape, _F32),
        grid=(rows // HGRN_ROWS,),
        in_specs=[spec, spec, spec, spec, _resident_spec((1, D_MODEL)), _resident_spec((D_MODEL, D_MODEL))],
        out_specs=spec,
        compiler_params=_params("parallel"),
        name="hgrn_post",
    )(x, o_f, o_b, gs, nw, wout)


def _trunk(x, w):
    batch, seq, _ = x.shape
    x = x.reshape(batch * seq, D_MODEL)
    depth = w["norm_w"].shape[0]
    for layer in range(depth):
        last = layer == depth - 1
        j = layer // 2
        x = _ffn(x, w["norm_w"][layer, 0][None], w["ffn_gate"][layer, 0], w["ffn_up"][layer, 0],
                 w["ffn_down"][layer, 0], w["final_norm"][None], final_norm=False)
        if layer % 2 == 0:
            x = _sgu(x, w["norm_w"][layer, 1][None], w["sgu_w_in"][j], w["sgu_ln_g"][j][None],
                     w["sgu_ln_b"][j][None], w["sgu_w_s"][j], w["sgu_b_s"][j], w["sgu_w_out"][j])
        else:
            q, kf, kb, v, gs, bf, bb = _hgrn_pre(x, w["norm_w"][layer, 1][None], w["hgrn_w_in"][j],
                                                 w["hgrn_lb_raw"][0], w["hgrn_lb_raw"][1], layer=layer)
            o_f, o_b = _scan(q, kf, kb, v, bf, bb, batch=batch)
            x = _hgrn_post(x, o_f, o_b, gs, w["hgrn_norm_w"][j][None], w["hgrn_w_out"][j])
        x = _ffn(x, w["norm_w"][layer, 2][None], w["ffn_gate"][layer, 1], w["ffn_up"][layer, 1],
                 w["ffn_down"][layer, 1], w["final_norm"][None], final_norm=last)
    return x.reshape(batch, seq, D_MODEL)


def kernel(x_prompt, x_sample, norm_w, ffn_gate, ffn_up, ffn_down, sgu_w_in, sgu_ln_g, sgu_ln_b, sgu_w_s,
           sgu_b_s, sgu_w_out, hgrn_w_in, hgrn_lb_raw, hgrn_norm_w, hgrn_w_out, final_norm):
    w = {
        "norm_w": norm_w,
        "ffn_gate": ffn_gate.astype(_BF16),
        "ffn_up": ffn_up.astype(_BF16),
        "ffn_down": ffn_down.astype(_BF16),
        "sgu_w_in": sgu_w_in.astype(_BF16),
        "sgu_ln_g": sgu_ln_g,
        "sgu_ln_b": sgu_ln_b,
        "sgu_w_s": sgu_w_s.astype(_BF16),
        "sgu_b_s": jnp.repeat(jnp.swapaxes(sgu_b_s, 1, 2), SGU_GROUP_DIM, axis=2),
        "sgu_w_out": sgu_w_out.astype(_BF16),
        "hgrn_w_in": hgrn_w_in.astype(_BF16),
        "hgrn_lb_raw": hgrn_lb_raw,
        "hgrn_norm_w": hgrn_norm_w,
        "hgrn_w_out": hgrn_w_out.astype(_BF16),
        "final_norm": final_norm,
    }
    return _trunk(x_prompt, w), _trunk(x_sample, w)
```

```python
import functools

import jax
import jax.numpy as jnp
from jax import lax
from jax.experimental import pallas as pl
from jax.experimental.pallas import tpu as pltpu

D_MODEL = 1024
FFN_DIM = 2816
SGU_DIM = 3 * D_MODEL
SGU_GROUPS = 8
SGU_GROUP_DIM = SGU_DIM // SGU_GROUPS
SGU_CHUNK = 128
HEAD_DIM = 128
HEADS = D_MODEL // HEAD_DIM
EPS = 1e-6

SCAN_CHUNK = 64
SUB_BLOCK = 16
SUB_REF = SUB_BLOCK // 2
SCAN_SEQS = 2

FFN_ROWS = 512
FFN_HGRN_ROWS = 512
SGU_ROWS = 512
SGU_PART = 256
HGRN_ROWS = 256

V7X_VMEM_LIMIT_BYTES = 56 * 1024 * 1024

_F32 = jnp.float32
_BF16 = jnp.bfloat16
_NT = (((1,), (1,)), ((), ()))
_TN = (((0,), (0,)), ((), ()))


def _params(*semantics):
    return pltpu.CompilerParams(dimension_semantics=semantics, vmem_limit_bytes=V7X_VMEM_LIMIT_BYTES)


def _row_spec(rows, cols):
    return pl.BlockSpec((rows, cols), lambda i: (i, 0))


def _resident_spec(shape):
    zeros = (0,) * len(shape)
    return pl.BlockSpec(shape, lambda *_: zeros, pipeline_mode=pl.Buffered(1))


def _rmsnorm(x, w):
    return x * lax.rsqrt(jnp.mean(x * x, axis=-1, keepdims=True) + EPS) * w


def _sigmoid(x):
    return 1.0 / (1.0 + jnp.exp(-x))


def _dot(a, b):
    return jnp.dot(a, b, preferred_element_type=_F32)


def _hgrn_mix(o, gate, nw, wout):
    parts = []
    for h in range(HEADS):
        oh = o[:, h * HEAD_DIM:(h + 1) * HEAD_DIM]
        parts.append(oh * lax.rsqrt(jnp.mean(oh * oh, axis=-1, keepdims=True) + EPS))
    o = jnp.concatenate(parts, axis=1) * nw * gate.astype(_F32)
    return _dot(o.astype(_BF16), wout)


def _ffn_kernel(*refs, final_norm, after_hgrn):
    if after_hgrn:
        of_ref, ob_ref, gs_ref, hnw_ref, hwo_ref, x_ref, nw_ref, wg_ref, wu_ref, wd_ref, fw_ref, o_ref = refs
        x = x_ref[...] + _hgrn_mix(of_ref[...] + ob_ref[...], gs_ref[...], hnw_ref[...], hwo_ref[...])
    else:
        x_ref, nw_ref, wg_ref, wu_ref, wd_ref, fw_ref, o_ref = refs
        x = x_ref[...]
    h = _rmsnorm(x, nw_ref[...]).astype(_BF16)
    gate = _dot(h, wg_ref[...])
    up = _dot(h, wu_ref[...])
    act = (gate * _sigmoid(gate) * up).astype(_BF16)
    y = x + 0.5 * _dot(act, wd_ref[...])
    if final_norm:
        y = _rmsnorm(y, fw_ref[...])
    o_ref[...] = y


def _ffn(x, nw, wg, wu, wd, fw, *, final_norm, hgrn=None):
    rows = x.shape[0]
    step = FFN_ROWS if hgrn is None else FFN_HGRN_ROWS
    spec = _row_spec(step, D_MODEL)
    hgrn_specs = [] if hgrn is None else [spec, spec, spec, _resident_spec((1, D_MODEL)),
                                          _resident_spec((D_MODEL, D_MODEL))]
    return pl.pallas_call(
        functools.partial(_ffn_kernel, final_norm=final_norm, after_hgrn=hgrn is not None),
        out_shape=jax.ShapeDtypeStruct(x.shape, _F32),
        grid=(rows // step,),
        in_specs=hgrn_specs + [
            spec,
            _resident_spec((1, D_MODEL)),
            _resident_spec((D_MODEL, FFN_DIM)),
            _resident_spec((D_MODEL, FFN_DIM)),
            _resident_spec((FFN_DIM, D_MODEL)),
            _resident_spec((1, D_MODEL)),
        ],
        out_specs=spec,
        compiler_params=_params("parallel"),
        name="ffn" if hgrn is None else "hgrn_out_ffn",
    )(*(hgrn or ()), x, nw, wg, wu, wd, fw)


def _sgu_kernel(x_ref, nw_ref, win_ref, lng_ref, lnb_ref, ws_ref, bs_ref, wout_ref, o_ref):
    def project(n):
        rows = slice(n * SGU_PART, (n + 1) * SGU_PART)
        h = _rmsnorm(x_ref[rows, :], nw_ref[...]).astype(_BF16)
        z = _dot(h, win_ref[...])
        z = 0.5 * z * (1.0 + lax.erf(z * (2.0 ** -0.5)))
        u = z[:, :SGU_DIM]
        v = z[:, SGU_DIM:]
        mu = jnp.mean(v, axis=-1, keepdims=True)
        vc = v - mu
        v = vc * lax.rsqrt(jnp.mean(vc * vc, axis=-1, keepdims=True) + EPS) * lng_ref[...] + lnb_ref[...]
        return u.astype(_BF16), v.astype(_BF16)

    def mix(n, u, v):
        gated = []
        for c in range(SGU_PART // SGU_CHUNK):
            chunk = slice(c * SGU_CHUNK, (c + 1) * SGU_CHUNK)
            parts = []
            for g in range(SGU_GROUPS):
                cols = slice(g * SGU_GROUP_DIM, (g + 1) * SGU_GROUP_DIM)
                parts.append(_dot(ws_ref[g], v[chunk, cols]))
            s = jnp.concatenate(parts, axis=1) + bs_ref[...]
            gated.append((u[chunk].astype(_F32) * s).astype(_BF16))
        rows = slice(n * SGU_PART, (n + 1) * SGU_PART)
        o_ref[rows, :] = x_ref[rows, :] + _dot(jnp.concatenate(gated, axis=0), wout_ref[...])

    nparts = SGU_ROWS // SGU_PART
    pending = project(0)
    for n in range(1, nparts):
        following = project(n)
        mix(n - 1, *pending)
        pending = following
    mix(nparts - 1, *pending)


def _sgu(x, nw, win, lng, lnb, ws, bs, wout):
    rows = x.shape[0]
    return pl.pallas_call(
        _sgu_kernel,
        out_shape=jax.ShapeDtypeStruct(x.shape, _F32),
        grid=(rows // SGU_ROWS,),
        in_specs=[
            _row_spec(SGU_ROWS, D_MODEL),
            _resident_spec((1, D_MODEL)),
            _resident_spec((D_MODEL, 2 * SGU_DIM)),
            _resident_spec((1, SGU_DIM)),
            _resident_spec((1, SGU_DIM)),
            _resident_spec((SGU_GROUPS, SGU_CHUNK, SGU_CHUNK)),
            _resident_spec((SGU_CHUNK, SGU_DIM)),
            _resident_spec((SGU_DIM, D_MODEL)),
        ],
        out_specs=_row_spec(SGU_ROWS, D_MODEL),
        compiler_params=_params("parallel"),
        name="sgu",
    )(x, nw, win, lng, lnb, ws, bs, wout)


def _lower_bound(raw, layer):
    e = jnp.exp(raw - jnp.max(raw, axis=0, keepdims=True))
    p = e / jnp.sum(e, axis=0, keepdims=True)
    lb = jnp.zeros_like(p[0:1])
    for j in range(1, layer + 1):
        lb = lb + p[j:j + 1]
    return lb


def _chunk_cumsum(ones_bf16, x):
    hi = x.astype(_BF16)
    lo = (x - hi.astype(_F32)).astype(_BF16)
    return _dot(ones_bf16, hi) + _dot(ones_bf16, lo)


def _hgrn_pre_kernel(x_ref, nw_ref, win_ref, lbf_ref, lbb_ref,
                     q_ref, kf_ref, kb_ref, v_ref, gs_ref, bf_ref, bb_ref, *, layer):
    x = x_ref[...]
    h = _rmsnorm(x, nw_ref[...]).astype(_BF16)

    def project(part):
        return _dot(h, win_ref[:, part * D_MODEL:(part + 1) * D_MODEL])

    logit_f, logit_b = project(1), project(2)
    q = project(0)
    q_ref[...] = (q * _sigmoid(q)).astype(_BF16)
    v_ref[...] = project(3).astype(_BF16)
    g = project(4)
    gs_ref[...] = (g * _sigmoid(g)).astype(_BF16)

    row = lax.broadcasted_iota(jnp.int32, (HGRN_ROWS, HGRN_ROWS), 0)
    col = lax.broadcasted_iota(jnp.int32, (HGRN_ROWS, HGRN_ROWS), 1)
    same_chunk = (row // SCAN_CHUNK) == (col // SCAN_CHUNK)
    prefix = jnp.where(same_chunk & (col <= row), 1.0, 0.0).astype(_BF16)
    suffix = jnp.where(same_chunk & (col >= row), 1.0, 0.0).astype(_BF16)

    lb = _lower_bound(lbf_ref[...], layer)
    f = lb + (1.0 - lb) * _sigmoid(logit_f)
    kf_ref[...] = (1.0 - f).astype(_BF16)
    bf_ref[...] = _chunk_cumsum(prefix, jnp.log2(f))

    lb = _lower_bound(lbb_ref[...], layer)
    f = lb + (1.0 - lb) * _sigmoid(logit_b)
    kb_ref[...] = (1.0 - f).astype(_BF16)
    bb_ref[...] = _chunk_cumsum(suffix, jnp.log2(f))


def _hgrn_pre(x, nw, win, lbf_raw, lbb_raw, *, layer):
    rows = x.shape[0]
    depth = lbf_raw.shape[0]
    half = jax.ShapeDtypeStruct(x.shape, _BF16)
    full = jax.ShapeDtypeStruct(x.shape, _F32)
    spec = _row_spec(HGRN_ROWS, D_MODEL)
    return pl.pallas_call(
        functools.partial(_hgrn_pre_kernel, layer=layer),
        out_shape=(half, half, half, half, half, full, full),
        grid=(rows // HGRN_ROWS,),
        in_specs=[
            spec,
            _resident_spec((1, D_MODEL)),
            _resident_spec((D_MODEL, 5 * D_MODEL)),
            _resident_spec((depth, D_MODEL)),
            _resident_spec((depth, D_MODEL)),
        ],
        out_specs=(spec,) * 7,
        compiler_params=_params("parallel"),
        name="hgrn_pre",
    )(x, nw, win, lbf_raw, lbb_raw)


def _scan_scores(q_ref, k_ref, v_ref, b_ref, st_ref, seq, head, slot, reverse):
    nsub = SCAN_CHUNK // SUB_BLOCK
    exit_row = 0 if reverse else SCAN_CHUNK - 1
    hs = slice(head * HEAD_DIM, (head + 1) * HEAD_DIM)
    b = b_ref[seq, :, hs]
    q = q_ref[seq, :, hs]
    k = k_ref[seq, :, hs]
    v = v_ref[seq, :, hs]
    b_exit = b[exit_row:exit_row + 1]
    state = st_ref[slot]
    q_in = q * jnp.exp2(b).astype(_BF16)
    o_inter = lax.dot_general(q_in, state.astype(_BF16), _NT, preferred_element_type=_F32)
    zero_block = jnp.zeros((SUB_BLOCK, HEAD_DIM), _BF16)
    q_blocks, k_blocks = [], []
    for blk in range(nsub):
        rs = slice(blk * SUB_BLOCK, (blk + 1) * SUB_BLOCK)
        cs = slice(blk * SUB_BLOCK, SCAN_CHUNK) if reverse else slice(0, (blk + 1) * SUB_BLOCK)
        b_mid = b[blk * SUB_BLOCK + SUB_REF:blk * SUB_BLOCK + SUB_REF + 1]
        q_row = [zero_block] * nsub
        q_row[blk] = q[rs] * jnp.exp2(b[rs] - b_mid).astype(_BF16)
        q_blocks.append(jnp.concatenate(q_row, axis=1))
        k_col = [zero_block] * nsub
        k_col[cs.start // SUB_BLOCK:cs.stop // SUB_BLOCK] = [k[cs] * jnp.exp2(b_mid - b[cs]).astype(_BF16)]
        k_blocks.append(jnp.concatenate(k_col, axis=0))
    scores = lax.dot_general(jnp.concatenate(q_blocks, axis=0), jnp.concatenate(k_blocks, axis=1), _NT,
                             preferred_element_type=_F32)
    k_out = k * jnp.exp2(b_exit - b).astype(_BF16)
    update = lax.dot_general(v, k_out, _TN, preferred_element_type=_F32)
    st_ref[slot] = state * jnp.exp2(b_exit) + update
    return scores, o_inter, v


def _scan_kernel(qf_ref, kf_ref, vf_ref, bf_ref, qb_ref, kb_ref, vb_ref, bb_ref, of_ref, ob_ref, st_ref):
    @pl.when(pl.program_id(1) == 0)
    def _():
        st_ref[...] = jnp.zeros_like(st_ref)

    t_idx = lax.broadcasted_iota(jnp.int32, (SCAN_CHUNK, SCAN_CHUNK), 0)
    s_idx = lax.broadcasted_iota(jnp.int32, (SCAN_CHUNK, SCAN_CHUNK), 1)
    streams = []
    for seq in range(SCAN_SEQS):
        streams.append((seq, False, (qf_ref, kf_ref, vf_ref, bf_ref), of_ref, s_idx <= t_idx))
        streams.append((seq, True, (qb_ref, kb_ref, vb_ref, bb_ref), ob_ref, s_idx >= t_idx))

    def stage1(index):
        seq, reverse, refs, _, _ = streams[index]
        return [_scan_scores(*refs, st_ref, seq, h, index * HEADS + h, reverse) for h in range(HEADS)]

    def stage2(index, staged):
        seq, _, _, o_ref, seen = streams[index]
        for h, (scores, o_inter, v) in enumerate(staged):
            o_ref[seq, :, h * HEAD_DIM:(h + 1) * HEAD_DIM] = (
                o_inter + _dot(jnp.where(seen, scores, 0.0).astype(_BF16), v))

    staged = stage1(0)
    for index in range(1, len(streams)):
        following = stage1(index)
        stage2(index - 1, staged)
        staged = following
    stage2(len(streams) - 1, staged)


def _scan(q, kf, kb, v, bf, bb, *, batch):
    seq = q.shape[0] // batch
    nblocks = seq // SCAN_CHUNK
    shape3 = (batch, seq, D_MODEL)
    block = (SCAN_SEQS, SCAN_CHUNK, D_MODEL)
    fwd = pl.BlockSpec(block, lambda i, n: (i, n, 0))
    bwd = pl.BlockSpec(block, lambda i, n: (i, nblocks - 1 - n, 0))
    q, kf, kb, v, bf, bb = (a.reshape(shape3) for a in (q, kf, kb, v, bf, bb))
    out = jax.ShapeDtypeStruct(shape3, _F32)
    o_f, o_b = pl.pallas_call(
        _scan_kernel,
        out_shape=(out, out),
        grid=(batch // SCAN_SEQS, nblocks),
        in_specs=[fwd] * 4 + [bwd] * 4,
        out_specs=(fwd, bwd),
        scratch_shapes=[pltpu.VMEM((2 * SCAN_SEQS * HEADS, HEAD_DIM, HEAD_DIM), _F32)],
        compiler_params=_params("parallel", "arbitrary"),
        name="scan",
    )(q, kf, v, bf, q, kb, v, bb)
    return o_f.reshape(batch * seq, D_MODEL), o_b.reshape(batch * seq, D_MODEL)


def _trunk(x, w):
    batch, seq, _ = x.shape
    x = x.reshape(batch * seq, D_MODEL)
    depth = w["norm_w"].shape[0]
    for layer in range(depth):
        last = layer == depth - 1
        j = layer // 2
        x = _ffn(x, w["norm_w"][layer, 0][None], w["ffn_gate"][layer, 0], w["ffn_up"][layer, 0],
                 w["ffn_down"][layer, 0], w["final_norm"][None], final_norm=False)
        hgrn = None
        if layer % 2 == 0:
            x = _sgu(x, w["norm_w"][layer, 1][None], w["sgu_w_in"][j], w["sgu_ln_g"][j][None],
                     w["sgu_ln_b"][j][None], w["sgu_w_s"][j], w["sgu_b_s"][j], w["sgu_w_out"][j])
        else:
            q, kf, kb, v, gs, bf, bb = _hgrn_pre(x, w["norm_w"][layer, 1][None], w["hgrn_w_in"][j],
                                                 w["hgrn_lb_raw"][0], w["hgrn_lb_raw"][1], layer=layer)
            o_f, o_b = _scan(q, kf, kb, v, bf, bb, batch=batch)
            hgrn = (o_f, o_b, gs, w["hgrn_norm_w"][j][None], w["hgrn_w_out"][j])
        x = _ffn(x, w["norm_w"][layer, 2][None], w["ffn_gate"][layer, 1], w["ffn_up"][layer, 1],
                 w["ffn_down"][layer, 1], w["final_norm"][None], final_norm=last, hgrn=hgrn)
    return x.reshape(batch, seq, D_MODEL)


def kernel(x_prompt, x_sample, norm_w, ffn_gate, ffn_up, ffn_down, sgu_w_in, sgu_ln_g, sgu_ln_b, sgu_w_s,
           sgu_b_s, sgu_w_out, hgrn_w_in, hgrn_lb_raw, hgrn_norm_w, hgrn_w_out, final_norm):
    w = {
        "norm_w": norm_w,
        "ffn_gate": ffn_gate.astype(_BF16),
        "ffn_up": ffn_up.astype(_BF16),
        "ffn_down": ffn_down.astype(_BF16),
        "sgu_w_in": sgu_w_in.astype(_BF16),
        "sgu_ln_g": sgu_ln_g,
        "sgu_ln_b": sgu_ln_b,
        "sgu_w_s": sgu_w_s.astype(_BF16),
        "sgu_b_s": jnp.repeat(jnp.swapaxes(sgu_b_s, 1, 2), SGU_GROUP_DIM, axis=2),
        "sgu_w_out": sgu_w_out.astype(_BF16),
        "hgrn_w_in": hgrn_w_in.astype(_BF16),
        "hgrn_lb_raw": hgrn_lb_raw,
        "hgrn_norm_w": hgrn_norm_w,
        "hgrn_w_out": hgrn_w_out.astype(_BF16),
        "final_norm": final_norm,
    }
    return _trunk(x_prompt, w), _trunk(x_sample, w)
```

```python
import functools

import jax
import jax.numpy as jnp
from jax import lax
from jax.experimental import pallas as pl
from jax.experimental.pallas import tpu as pltpu

D_MODEL = 1024
FFN_DIM = 2816
SGU_DIM = 3 * D_MODEL
SGU_GROUPS = 8
SGU_GROUP_DIM = SGU_DIM // SGU_GROUPS
SGU_CHUNK = 128
HEAD_DIM = 128
HEADS = D_MODEL // HEAD_DIM
EPS = 1e-6

SCAN_CHUNK = 64
SUB_BLOCK = 16
SUB_REF = SUB_BLOCK // 2
SCORE_GROUP = 2
SCAN_SEQS = 4
SCAN_LAG = 3

FFN_ROWS = 512
FFN_PART = 256
FFN_HGRN_ROWS = 512
SGU_ROWS = 512
SGU_PART = 256
HGRN_ROWS = 512
HGRN_PART = 256

V7X_VMEM_LIMIT_BYTES = 56 * 1024 * 1024

_F32 = jnp.float32
_BF16 = jnp.bfloat16
_NT = (((1,), (1,)), ((), ()))
_TN = (((0,), (0,)), ((), ()))


def _params(*semantics):
    return pltpu.CompilerParams(dimension_semantics=semantics, vmem_limit_bytes=V7X_VMEM_LIMIT_BYTES)


def _row_spec(rows, cols):
    return pl.BlockSpec((rows, cols), lambda i: (i, 0))


def _resident_spec(shape):
    zeros = (0,) * len(shape)
    return pl.BlockSpec(shape, lambda *_: zeros, pipeline_mode=pl.Buffered(1))


def _rmsnorm(x, w):
    return x * lax.rsqrt(jnp.mean(x * x, axis=-1, keepdims=True) + EPS) * w


def _sigmoid(x):
    return 1.0 / (1.0 + jnp.exp(-x))


def _dot(a, b):
    return jnp.dot(a, b, preferred_element_type=_F32)


def _hgrn_mix(o, gate, nw, wout):
    parts = []
    for h in range(HEADS):
        oh = o[:, h * HEAD_DIM:(h + 1) * HEAD_DIM]
        parts.append(oh * lax.rsqrt(jnp.mean(oh * oh, axis=-1, keepdims=True) + EPS))
    o = jnp.concatenate(parts, axis=1) * nw * gate.astype(_F32)
    return _dot(o.astype(_BF16), wout)


def _ffn_kernel(*refs, final_norm, after_hgrn):
    if after_hgrn:
        of_ref, ob_ref, gs_ref, hnw_ref, hwo_ref, x_ref, nw_ref, wg_ref, wu_ref, wd_ref, fw_ref, o_ref = refs
    else:
        x_ref, nw_ref, wg_ref, wu_ref, wd_ref, fw_ref, o_ref = refs

    def expand(rows):
        x = x_ref[rows, :]
        if after_hgrn:
            x = x + _hgrn_mix(of_ref[rows, :] + ob_ref[rows, :], gs_ref[rows, :], hnw_ref[...], hwo_ref[...])
        h = _rmsnorm(x, nw_ref[...]).astype(_BF16)
        gate = _dot(h, wg_ref[...])
        up = _dot(h, wu_ref[...])
        return x, (gate * _sigmoid(gate) * up).astype(_BF16)

    def contract(rows, x, act):
        y = x + 0.5 * _dot(act, wd_ref[...])
        if final_norm:
            y = _rmsnorm(y, fw_ref[...])
        o_ref[rows, :] = y

    parts = [slice(n * FFN_PART, (n + 1) * FFN_PART) for n in range(x_ref.shape[0] // FFN_PART)]
    pending = expand(parts[0])
    for previous, rows in zip(parts, parts[1:]):
        following = expand(rows)
        contract(previous, *pending)
        pending = following
    contract(parts[-1], *pending)


def _ffn(x, nw, wg, wu, wd, fw, *, final_norm, hgrn=None):
    rows = x.shape[0]
    step = FFN_ROWS if hgrn is None else FFN_HGRN_ROWS
    spec = _row_spec(step, D_MODEL)
    hgrn_specs = [] if hgrn is None else [spec, spec, spec, _resident_spec((1, D_MODEL)),
                                          _resident_spec((D_MODEL, D_MODEL))]
    return pl.pallas_call(
        functools.partial(_ffn_kernel, final_norm=final_norm, after_hgrn=hgrn is not None),
        out_shape=jax.ShapeDtypeStruct(x.shape, _F32),
        grid=(rows // step,),
        in_specs=hgrn_specs + [
            spec,
            _resident_spec((1, D_MODEL)),
            _resident_spec((D_MODEL, FFN_DIM)),
            _resident_spec((D_MODEL, FFN_DIM)),
            _resident_spec((FFN_DIM, D_MODEL)),
            _resident_spec((1, D_MODEL)),
        ],
        out_specs=spec,
        compiler_params=_params("parallel"),
        name="ffn" if hgrn is None else "hgrn_out_ffn",
    )(*(hgrn or ()), x, nw, wg, wu, wd, fw)


def _sgu_kernel(x_ref, nw_ref, win_ref, lng_ref, lnb_ref, ws_ref, bs_ref, wout_ref, o_ref):
    def project(n):
        rows = slice(n * SGU_PART, (n + 1) * SGU_PART)
        h = _rmsnorm(x_ref[rows, :], nw_ref[...]).astype(_BF16)
        z = _dot(h, win_ref[...])
        z = 0.5 * z * (1.0 + lax.erf(z * (2.0 ** -0.5)))
        u = z[:, :SGU_DIM]
        v = z[:, SGU_DIM:]
        mu = jnp.mean(v, axis=-1, keepdims=True)
        vc = v - mu
        v = vc * lax.rsqrt(jnp.mean(vc * vc, axis=-1, keepdims=True) + EPS) * lng_ref[...] + lnb_ref[...]
        return u.astype(_BF16), v.astype(_BF16)

    def mix(n, u, v):
        gated = []
        for c in range(SGU_PART // SGU_CHUNK):
            chunk = slice(c * SGU_CHUNK, (c + 1) * SGU_CHUNK)
            parts = []
            for g in range(SGU_GROUPS):
                cols = slice(g * SGU_GROUP_DIM, (g + 1) * SGU_GROUP_DIM)
                parts.append(_dot(ws_ref[g], v[chunk, cols]))
            s = jnp.concatenate(parts, axis=1) + bs_ref[...]
            gated.append((u[chunk].astype(_F32) * s).astype(_BF16))
        rows = slice(n * SGU_PART, (n + 1) * SGU_PART)
        o_ref[rows, :] = x_ref[rows, :] + _dot(jnp.concatenate(gated, axis=0), wout_ref[...])

    nparts = SGU_ROWS // SGU_PART
    pending = project(0)
    for n in range(1, nparts):
        following = project(n)
        mix(n - 1, *pending)
        pending = following
    mix(nparts - 1, *pending)


def _sgu(x, nw, win, lng, lnb, ws, bs, wout):
    rows = x.shape[0]
    return pl.pallas_call(
        _sgu_kernel,
        out_shape=jax.ShapeDtypeStruct(x.shape, _F32),
        grid=(rows // SGU_ROWS,),
        in_specs=[
            _row_spec(SGU_ROWS, D_MODEL),
            _resident_spec((1, D_MODEL)),
            _resident_spec((D_MODEL, 2 * SGU_DIM)),
            _resident_spec((1, SGU_DIM)),
            _resident_spec((1, SGU_DIM)),
            _resident_spec((SGU_GROUPS, SGU_CHUNK, SGU_CHUNK)),
            _resident_spec((SGU_CHUNK, SGU_DIM)),
            _resident_spec((SGU_DIM, D_MODEL)),
        ],
        out_specs=_row_spec(SGU_ROWS, D_MODEL),
        compiler_params=_params("parallel"),
        name="sgu",
    )(x, nw, win, lng, lnb, ws, bs, wout)


def _lower_bound(raw, layer):
    e = jnp.exp(raw - jnp.max(raw, axis=0, keepdims=True))
    p = e / jnp.sum(e, axis=0, keepdims=True)
    lb = jnp.zeros_like(p[0:1])
    for j in range(1, layer + 1):
        lb = lb + p[j:j + 1]
    return lb


def _chunk_cumsum(ones_bf16, x):
    hi = x.astype(_BF16)
    lo = (x - hi.astype(_F32)).astype(_BF16)
    return _dot(ones_bf16, hi) + _dot(ones_bf16, lo)


def _hgrn_pre_kernel(x_ref, nw_ref, win_ref, lbf_ref, lbb_ref,
                     q_ref, kf_ref, kb_ref, v_ref, gs_ref, bf_ref, bb_ref, *, layer):
    row = lax.broadcasted_iota(jnp.int32, (HGRN_PART, HGRN_PART), 0)
    col = lax.broadcasted_iota(jnp.int32, (HGRN_PART, HGRN_PART), 1)
    same_chunk = (row // SCAN_CHUNK) == (col // SCAN_CHUNK)
    prefix = jnp.where(same_chunk & (col <= row), 1.0, 0.0).astype(_BF16)
    suffix = jnp.where(same_chunk & (col >= row), 1.0, 0.0).astype(_BF16)
    lb_f = _lower_bound(lbf_ref[...], layer)
    lb_b = _lower_bound(lbb_ref[...], layer)

    def project(rows):
        h = _rmsnorm(x_ref[rows, :], nw_ref[...]).astype(_BF16)

        def columns(part):
            return _dot(h, win_ref[:, part * D_MODEL:(part + 1) * D_MODEL])

        logit_f, logit_b = columns(1), columns(2)
        q = columns(0)
        q_ref[rows, :] = (q * _sigmoid(q)).astype(_BF16)
        v_ref[rows, :] = columns(3).astype(_BF16)
        g = columns(4)
        gs_ref[rows, :] = (g * _sigmoid(g)).astype(_BF16)
        f_f = lb_f + (1.0 - lb_f) * _sigmoid(logit_f)
        kf_ref[rows, :] = (1.0 - f_f).astype(_BF16)
        f_b = lb_b + (1.0 - lb_b) * _sigmoid(logit_b)
        kb_ref[rows, :] = (1.0 - f_b).astype(_BF16)
        return jnp.log2(f_f), jnp.log2(f_b)

    def accumulate(rows, log_f, log_b):
        bf_ref[rows, :] = _chunk_cumsum(prefix, log_f)
        bb_ref[rows, :] = _chunk_cumsum(suffix, log_b)

    parts = [slice(n * HGRN_PART, (n + 1) * HGRN_PART) for n in range(HGRN_ROWS // HGRN_PART)]
    pending = project(parts[0])
    for previous, rows in zip(parts, parts[1:]):
        following = project(rows)
        accumulate(previous, *pending)
        pending = following
    accumulate(parts[-1], *pending)


def _hgrn_pre(x, nw, win, lbf_raw, lbb_raw, *, layer):
    rows = x.shape[0]
    depth = lbf_raw.shape[0]
    half = jax.ShapeDtypeStruct(x.shape, _BF16)
    full = jax.ShapeDtypeStruct(x.shape, _F32)
    spec = _row_spec(HGRN_ROWS, D_MODEL)
    return pl.pallas_call(
        functools.partial(_hgrn_pre_kernel, layer=layer),
        out_shape=(half, half, half, half, half, full, full),
        grid=(rows // HGRN_ROWS,),
        in_specs=[
            spec,
            _resident_spec((1, D_MODEL)),
            _resident_spec((D_MODEL, 5 * D_MODEL)),
            _resident_spec((depth, D_MODEL)),
            _resident_spec((depth, D_MODEL)),
        ],
        out_specs=(spec,) * 7,
        compiler_params=_params("parallel"),
        name="hgrn_pre",
    )(x, nw, win, lbf_raw, lbb_raw)


def _scan_scores(q_ref, k_ref, v_ref, b_ref, st_ref, seq, head, slot, reverse):
    nsub = SCAN_CHUNK // SUB_BLOCK
    exit_row = 0 if reverse else SCAN_CHUNK - 1
    hs = slice(head * HEAD_DIM, (head + 1) * HEAD_DIM)
    b = b_ref[seq, :, hs]
    q = q_ref[seq, :, hs]
    k = k_ref[seq, :, hs]
    v = v_ref[seq, :, hs]
    b_exit = b[exit_row:exit_row + 1]
    state = st_ref[slot]
    q_in = q * jnp.exp2(b).astype(_BF16)
    o_inter = lax.dot_general(q_in, state.astype(_BF16), _NT, preferred_element_type=_F32)
    zero_block = jnp.zeros((SUB_BLOCK, HEAD_DIM), _BF16)
    score_rows = []
    for first in range(0, nsub, SCORE_GROUP):
        q_blocks, k_blocks = [], []
        for slot_in_group in range(SCORE_GROUP):
            blk = first + slot_in_group
            rs = slice(blk * SUB_BLOCK, (blk + 1) * SUB_BLOCK)
            cs = slice(blk * SUB_BLOCK, SCAN_CHUNK) if reverse else slice(0, (blk + 1) * SUB_BLOCK)
            b_mid = b[blk * SUB_BLOCK + SUB_REF:blk * SUB_BLOCK + SUB_REF + 1]
            q_row = [zero_block] * SCORE_GROUP
            q_row[slot_in_group] = q[rs] * jnp.exp2(b[rs] - b_mid).astype(_BF16)
            q_blocks.append(jnp.concatenate(q_row, axis=1))
            k_col = [zero_block] * nsub
            k_col[cs.start // SUB_BLOCK:cs.stop // SUB_BLOCK] = [k[cs] * jnp.exp2(b_mid - b[cs]).astype(_BF16)]
            k_blocks.append(jnp.concatenate(k_col, axis=0))
        score_rows.append(lax.dot_general(jnp.concatenate(q_blocks, axis=0), jnp.concatenate(k_blocks, axis=1),
                                          _NT, preferred_element_type=_F32))
    scores = jnp.concatenate(score_rows, axis=0)
    k_out = k * jnp.exp2(b_exit - b).astype(_BF16)
    update = lax.dot_general(v, k_out, _TN, preferred_element_type=_F32)
    st_ref[slot] = state * jnp.exp2(b_exit) + update
    return scores, o_inter, v


def _scan_kernel(qf_ref, kf_ref, vf_ref, bf_ref, qb_ref, kb_ref, vb_ref, bb_ref, of_ref, ob_ref, st_ref):
    @pl.when(pl.program_id(1) == 0)
    def _():
        st_ref[...] = jnp.zeros_like(st_ref)

    t_idx = lax.broadcasted_iota(jnp.int32, (SCAN_CHUNK, SCAN_CHUNK), 0)
    s_idx = lax.broadcasted_iota(jnp.int32, (SCAN_CHUNK, SCAN_CHUNK), 1)
    streams = []
    for seq in range(SCAN_SEQS):
        streams.append((seq, False, (qf_ref, kf_ref, vf_ref, bf_ref), of_ref, s_idx <= t_idx))
        streams.append((seq, True, (qb_ref, kb_ref, vb_ref, bb_ref), ob_ref, s_idx >= t_idx))

    items = [(index, h) for index in range(len(streams)) for h in range(HEADS)]

    def stage1(index, h):
        seq, reverse, refs, _, _ = streams[index]
        return _scan_scores(*refs, st_ref, seq, h, index * HEADS + h, reverse)

    def stage2(index, h, scores, o_inter, v):
        seq, _, _, o_ref, seen = streams[index]
        o_ref[seq, :, h * HEAD_DIM:(h + 1) * HEAD_DIM] = (
            o_inter + _dot(jnp.where(seen, scores, 0.0).astype(_BF16), v))

    staged = {}
    for step in range(len(items) + SCAN_LAG):
        if step < len(items):
            staged[step] = stage1(*items[step])
        if step >= SCAN_LAG:
            stage2(*items[step - SCAN_LAG], *staged.pop(step - SCAN_LAG))


def _scan(q, kf, kb, v, bf, bb, *, batch):
    seq = q.shape[0] // batch
    nblocks = seq // SCAN_CHUNK
    shape3 = (batch, seq, D_MODEL)
    block = (SCAN_SEQS, SCAN_CHUNK, D_MODEL)
    fwd = pl.BlockSpec(block, lambda i, n: (i, n, 0))
    bwd = pl.BlockSpec(block, lambda i, n: (i, nblocks - 1 - n, 0))
    q, kf, kb, v, bf, bb = (a.reshape(shape3) for a in (q, kf, kb, v, bf, bb))
    out = jax.ShapeDtypeStruct(shape3, _F32)
    o_f, o_b = pl.pallas_call(
        _scan_kernel,
        out_shape=(out, out),
        grid=(batch // SCAN_SEQS, nblocks),
        in_specs=[fwd] * 4 + [bwd] * 4,
        out_specs=(fwd, bwd),
        scratch_shapes=[pltpu.VMEM((2 * SCAN_SEQS * HEADS, HEAD_DIM, HEAD_DIM), _F32)],
        compiler_params=_params("parallel", "arbitrary"),
        name="scan",
    )(q, kf, v, bf, q, kb, v, bb)
    return o_f.reshape(batch * seq, D_MODEL), o_b.reshape(batch * seq, D_MODEL)


def _trunk(x, w):
    batch, seq, _ = x.shape
    x = x.reshape(batch * seq, D_MODEL)
    depth = w["norm_w"].shape[0]
    for layer in range(depth):
        last = layer == depth - 1
        j = layer // 2
        x = _ffn(x, w["norm_w"][layer, 0][None], w["ffn_gate"][layer, 0], w["ffn_up"][layer, 0],
                 w["ffn_down"][layer, 0], w["final_norm"][None], final_norm=False)
        hgrn = None
        if layer % 2 == 0:
            x = _sgu(x, w["norm_w"][layer, 1][None], w["sgu_w_in"][j], w["sgu_ln_g"][j][None],
                     w["sgu_ln_b"][j][None], w["sgu_w_s"][j], w["sgu_b_s"][j], w["sgu_w_out"][j])
        else:
            q, kf, kb, v, gs, bf, bb = _hgrn_pre(x, w["norm_w"][layer, 1][None], w["hgrn_w_in"][j],
                                                 w["hgrn_lb_raw"][0], w["hgrn_lb_raw"][1], layer=layer)
            o_f, o_b = _scan(q, kf, kb, v, bf, bb, batch=batch)
            hgrn = (o_f, o_b, gs, w["hgrn_norm_w"][j][None], w["hgrn_w_out"][j])
        x = _ffn(x, w["norm_w"][layer, 2][None], w["ffn_gate"][layer, 1], w["ffn_up"][layer, 1],
                 w["ffn_down"][layer, 1], w["final_norm"][None], final_norm=last, hgrn=hgrn)
    return x.reshape(batch, seq, D_MODEL)


def kernel(x_prompt, x_sample, norm_w, ffn_gate, ffn_up, ffn_down, sgu_w_in, sgu_ln_g, sgu_ln_b, sgu_w_s,
           sgu_b_s, sgu_w_out, hgrn_w_in, hgrn_lb_raw, hgrn_norm_w, hgrn_w_out, final_norm):
    w = {
        "norm_w": norm_w,
        "ffn_gate": ffn_gate.astype(_BF16),
        "ffn_up": ffn_up.astype(_BF16),
        "ffn_down": ffn_down.astype(_BF16),
        "sgu_w_in": sgu_w_in.astype(_BF16),
        "sgu_ln_g": sgu_ln_g,
        "sgu_ln_b": sgu_ln_b,
        "sgu_w_s": sgu_w_s.astype(_BF16),
        "sgu_b_s": jnp.repeat(jnp.swapaxes(sgu_b_s, 1, 2), SGU_GROUP_DIM, axis=2),
        "sgu_w_out": sgu_w_out.astype(_BF16),
        "hgrn_w_in": hgrn_w_in.astype(_BF16),
        "hgrn_lb_raw": hgrn_lb_raw,
        "hgrn_norm_w": hgrn_norm_w,
        "hgrn_w_out": hgrn_w_out.astype(_BF16),
        "final_norm": final_norm,
    }
    return _trunk(x_prompt, w), _trunk(x_sample, w)
```

```python
import functools

import jax
import jax.numpy as jnp
from jax import lax
from jax.experimental import pallas as pl
from jax.experimental.pallas import tpu as pltpu

D_MODEL = 1024
FFN_DIM = 2816
SGU_DIM = 3 * D_MODEL
SGU_GROUPS = 8
SGU_GROUP_DIM = SGU_DIM // SGU_GROUPS
SGU_CHUNK = 128
HEAD_DIM = 128
HEADS = D_MODEL // HEAD_DIM
EPS = 1e-6

SCAN_CHUNK = 64
SUB_BLOCK = 16
SUB_REF = SUB_BLOCK // 2
SCORE_GROUP = 2
SCAN_SEQS = 4
SCAN_LAG = 3

FFN_ROWS = 512
FFN_PART = 256
FFN_HGRN_ROWS = 512
SGU_ROWS = 512
SGU_PART = 256
HGRN_ROWS = 512
HGRN_PART = 256

V7X_VMEM_LIMIT_BYTES = 56 * 1024 * 1024

_F32 = jnp.float32
_BF16 = jnp.bfloat16
_NT = (((1,), (1,)), ((), ()))
_TN = (((0,), (0,)), ((), ()))


def _params(*semantics):
    return pltpu.CompilerParams(dimension_semantics=semantics, vmem_limit_bytes=V7X_VMEM_LIMIT_BYTES)


def _row_spec(rows, cols):
    return pl.BlockSpec((rows, cols), lambda i: (i, 0))


def _resident_spec(shape):
    zeros = (0,) * len(shape)
    return pl.BlockSpec(shape, lambda *_: zeros, pipeline_mode=pl.Buffered(1))


def _rmsnorm(x, w):
    return x * lax.rsqrt(jnp.mean(x * x, axis=-1, keepdims=True) + EPS) * w


def _sigmoid(x):
    return 1.0 / (1.0 + jnp.exp(-x))


def _dot(a, b):
    return jnp.dot(a, b, preferred_element_type=_F32)


def _hgrn_mix(o, gate, nw, wout):
    parts = []
    for h in range(HEADS):
        oh = o[:, h * HEAD_DIM:(h + 1) * HEAD_DIM]
        parts.append(oh * lax.rsqrt(jnp.mean(oh * oh, axis=-1, keepdims=True) + EPS))
    o = jnp.concatenate(parts, axis=1) * nw * gate.astype(_F32)
    return _dot(o.astype(_BF16), wout)


def _ffn_kernel(*refs, final_norm, after_hgrn):
    if after_hgrn:
        of_ref, ob_ref, gs_ref, hnw_ref, hwo_ref, x_ref, nw_ref, wg_ref, wu_ref, wd_ref, fw_ref, o_ref = refs
    else:
        x_ref, nw_ref, wg_ref, wu_ref, wd_ref, fw_ref, o_ref = refs

    def residual(rows):
        x = x_ref[rows, :]
        if after_hgrn:
            x = x + _hgrn_mix(of_ref[rows, :] + ob_ref[rows, :], gs_ref[rows, :], hnw_ref[...], hwo_ref[...])
        return x

    def expand(x):
        h = _rmsnorm(x, nw_ref[...]).astype(_BF16)
        gate = _dot(h, wg_ref[...])
        up = _dot(h, wu_ref[...])
        return x, (gate * _sigmoid(gate) * up).astype(_BF16)

    def contract(rows, x, act):
        y = x + 0.5 * _dot(act, wd_ref[...])
        if final_norm:
            y = _rmsnorm(y, fw_ref[...])
        o_ref[rows, :] = y

    parts = [slice(n * FFN_PART, (n + 1) * FFN_PART) for n in range(x_ref.shape[0] // FFN_PART)]
    inputs = [residual(rows) for rows in parts]
    pending = expand(inputs[0])
    for previous, x in zip(parts, inputs[1:]):
        following = expand(x)
        contract(previous, *pending)
        pending = following
    contract(parts[-1], *pending)


def _ffn(x, nw, wg, wu, wd, fw, *, final_norm, hgrn=None):
    rows = x.shape[0]
    step = FFN_ROWS if hgrn is None else FFN_HGRN_ROWS
    spec = _row_spec(step, D_MODEL)
    hgrn_specs = [] if hgrn is None else [spec, spec, spec, _resident_spec((1, D_MODEL)),
                                          _resident_spec((D_MODEL, D_MODEL))]
    return pl.pallas_call(
        functools.partial(_ffn_kernel, final_norm=final_norm, after_hgrn=hgrn is not None),
        out_shape=jax.ShapeDtypeStruct(x.shape, _F32),
        grid=(rows // step,),
        in_specs=hgrn_specs + [
            spec,
            _resident_spec((1, D_MODEL)),
            _resident_spec((D_MODEL, FFN_DIM)),
            _resident_spec((D_MODEL, FFN_DIM)),
            _resident_spec((FFN_DIM, D_MODEL)),
            _resident_spec((1, D_MODEL)),
        ],
        out_specs=spec,
        compiler_params=_params("parallel"),
        name="ffn" if hgrn is None else "hgrn_out_ffn",
    )(*(hgrn or ()), x, nw, wg, wu, wd, fw)


def _sgu_kernel(x_ref, nw_ref, win_ref, lng_ref, lnb_ref, ws_ref, bs_ref, wout_ref, o_ref):
    def project(n):
        rows = slice(n * SGU_PART, (n + 1) * SGU_PART)
        h = _rmsnorm(x_ref[rows, :], nw_ref[...]).astype(_BF16)
        def gelu(z):
            return 0.5 * z * (1.0 + lax.erf(z * (2.0 ** -0.5)))

        v = gelu(_dot(h, win_ref[:, SGU_DIM:]))
        u = gelu(_dot(h, win_ref[:, :SGU_DIM]))
        mu = jnp.mean(v, axis=-1, keepdims=True)
        vc = v - mu
        v = vc * lax.rsqrt(jnp.mean(vc * vc, axis=-1, keepdims=True) + EPS) * lng_ref[...] + lnb_ref[...]
        return u.astype(_BF16), v.astype(_BF16)

    def mix(n, u, v):
        gated = []
        for c in range(SGU_PART // SGU_CHUNK):
            chunk = slice(c * SGU_CHUNK, (c + 1) * SGU_CHUNK)
            parts = []
            for g in range(SGU_GROUPS):
                cols = slice(g * SGU_GROUP_DIM, (g + 1) * SGU_GROUP_DIM)
                parts.append(_dot(ws_ref[g], v[chunk, cols]))
            s = jnp.concatenate(parts, axis=1) + bs_ref[...]
            gated.append((u[chunk].astype(_F32) * s).astype(_BF16))
        rows = slice(n * SGU_PART, (n + 1) * SGU_PART)
        o_ref[rows, :] = x_ref[rows, :] + _dot(jnp.concatenate(gated, axis=0), wout_ref[...])

    nparts = SGU_ROWS // SGU_PART
    pending = project(0)
    for n in range(1, nparts):
        following = project(n)
        mix(n - 1, *pending)
        pending = following
    mix(nparts - 1, *pending)


def _sgu(x, nw, win, lng, lnb, ws, bs, wout):
    rows = x.shape[0]
    return pl.pallas_call(
        _sgu_kernel,
        out_shape=jax.ShapeDtypeStruct(x.shape, _F32),
        grid=(rows // SGU_ROWS,),
        in_specs=[
            _row_spec(SGU_ROWS, D_MODEL),
            _resident_spec((1, D_MODEL)),
            _resident_spec((D_MODEL, 2 * SGU_DIM)),
            _resident_spec((1, SGU_DIM)),
            _resident_spec((1, SGU_DIM)),
            _resident_spec((SGU_GROUPS, SGU_CHUNK, SGU_CHUNK)),
            _resident_spec((SGU_CHUNK, SGU_DIM)),
            _resident_spec((SGU_DIM, D_MODEL)),
        ],
        out_specs=_row_spec(SGU_ROWS, D_MODEL),
        compiler_params=_params("parallel"),
        name="sgu",
    )(x, nw, win, lng, lnb, ws, bs, wout)


def _lower_bound(raw, layer):
    e = jnp.exp(raw - jnp.max(raw, axis=0, keepdims=True))
    p = e / jnp.sum(e, axis=0, keepdims=True)
    lb = jnp.zeros_like(p[0:1])
    for j in range(1, layer + 1):
        lb = lb + p[j:j + 1]
    return lb


def _chunk_cumsum(ones3_bf16, x):
    out = []
    for c in range(x.shape[0] // SCAN_CHUNK):
        xc = x[c * SCAN_CHUNK:(c + 1) * SCAN_CHUNK]
        hi = xc.astype(_BF16)
        r1 = xc - hi.astype(_F32)
        mid = r1.astype(_BF16)
        lo = (r1 - mid.astype(_F32)).astype(_BF16)
        out.append(_dot(ones3_bf16, jnp.concatenate([hi, mid, lo], axis=0)))
    return jnp.concatenate(out, axis=0)


def _hgrn_pre_kernel(x_ref, nw_ref, win_ref, lbf_ref, lbb_ref,
                     q_ref, kf_ref, kb_ref, v_ref, gs_ref, bf_ref, bb_ref, *, layer):
    row = lax.broadcasted_iota(jnp.int32, (SCAN_CHUNK, 3 * SCAN_CHUNK), 0)
    col = lax.broadcasted_iota(jnp.int32, (SCAN_CHUNK, 3 * SCAN_CHUNK), 1) % SCAN_CHUNK
    prefix = jnp.where(col <= row, 1.0, 0.0).astype(_BF16)
    suffix = jnp.where(col >= row, 1.0, 0.0).astype(_BF16)
    lb_f = _lower_bound(lbf_ref[...], layer)
    lb_b = _lower_bound(lbb_ref[...], layer)

    def project(rows):
        h = _rmsnorm(x_ref[rows, :], nw_ref[...]).astype(_BF16)

        def columns(part):
            return _dot(h, win_ref[:, part * D_MODEL:(part + 1) * D_MODEL])

        logit_f, logit_b = columns(1), columns(2)
        q = columns(0)
        q_ref[rows, :] = (q * _sigmoid(q)).astype(_BF16)
        v_ref[rows, :] = columns(3).astype(_BF16)
        g = columns(4)
        gs_ref[rows, :] = (g * _sigmoid(g)).astype(_BF16)
        f_f = lb_f + (1.0 - lb_f) * _sigmoid(logit_f)
        kf_ref[rows, :] = (1.0 - f_f).astype(_BF16)
        f_b = lb_b + (1.0 - lb_b) * _sigmoid(logit_b)
        kb_ref[rows, :] = (1.0 - f_b).astype(_BF16)
        return jnp.log2(f_f), jnp.log2(f_b)

    def accumulate(rows, log_f, log_b):
        bf_ref[rows, :] = _chunk_cumsum(prefix, log_f)
        bb_ref[rows, :] = _chunk_cumsum(suffix, log_b)

    parts = [slice(n * HGRN_PART, (n + 1) * HGRN_PART) for n in range(HGRN_ROWS // HGRN_PART)]
    pending = project(parts[0])
    for previous, rows in zip(parts, parts[1:]):
        following = project(rows)
        accumulate(previous, *pending)
        pending = following
    accumulate(parts[-1], *pending)


def _hgrn_pre(x, nw, win, lbf_raw, lbb_raw, *, layer):
    rows = x.shape[0]
    depth = lbf_raw.shape[0]
    half = jax.ShapeDtypeStruct(x.shape, _BF16)
    full = jax.ShapeDtypeStruct(x.shape, _F32)
    spec = _row_spec(HGRN_ROWS, D_MODEL)
    return pl.pallas_call(
        functools.partial(_hgrn_pre_kernel, layer=layer),
        out_shape=(half, half, half, half, half, full, full),
        grid=(rows // HGRN_ROWS,),
        in_specs=[
            spec,
            _resident_spec((1, D_MODEL)),
            _resident_spec((D_MODEL, 5 * D_MODEL)),
            _resident_spec((depth, D_MODEL)),
            _resident_spec((depth, D_MODEL)),
        ],
        out_specs=(spec,) * 7,
        compiler_params=_params("parallel"),
        name="hgrn_pre",
    )(x, nw, win, lbf_raw, lbb_raw)


def _scan_scores(q_ref, k_ref, v_ref, b_ref, st_ref, seq, head, slot, reverse):
    nsub = SCAN_CHUNK // SUB_BLOCK
    exit_row = 0 if reverse else SCAN_CHUNK - 1
    hs = slice(head * HEAD_DIM, (head + 1) * HEAD_DIM)
    b = b_ref[seq, :, hs]
    q = q_ref[seq, :, hs].astype(_F32)
    k = k_ref[seq, :, hs].astype(_F32)
    v = v_ref[seq, :, hs]
    b_exit = b[exit_row:exit_row + 1]
    state = st_ref[slot]
    q_in = (q * jnp.exp2(b)).astype(_BF16)
    o_inter = lax.dot_general(q_in, state.astype(_BF16), _NT, preferred_element_type=_F32)
    zero_block = jnp.zeros((SUB_BLOCK, HEAD_DIM), _BF16)
    score_rows = []
    for first in range(0, nsub, SCORE_GROUP):
        q_blocks, k_blocks = [], []
        for slot_in_group in range(SCORE_GROUP):
            blk = first + slot_in_group
            rs = slice(blk * SUB_BLOCK, (blk + 1) * SUB_BLOCK)
            cs = slice(blk * SUB_BLOCK, SCAN_CHUNK) if reverse else slice(0, (blk + 1) * SUB_BLOCK)
            b_mid = b[blk * SUB_BLOCK + SUB_REF:blk * SUB_BLOCK + SUB_REF + 1]
            q_row = [zero_block] * SCORE_GROUP
            q_row[slot_in_group] = (q[rs] * jnp.exp2(b[rs] - b_mid)).astype(_BF16)
            q_blocks.append(jnp.concatenate(q_row, axis=1))
            k_col = [zero_block] * nsub
            k_col[cs.start // SUB_BLOCK:cs.stop // SUB_BLOCK] = [(k[cs] * jnp.exp2(b_mid - b[cs])).astype(_BF16)]
            k_blocks.append(jnp.concatenate(k_col, axis=0))
        score_rows.append(lax.dot_general(jnp.concatenate(q_blocks, axis=0), jnp.concatenate(k_blocks, axis=1),
                                          _NT, preferred_element_type=_F32))
    scores = jnp.concatenate(score_rows, axis=0)
    k_out = (k * jnp.exp2(b_exit - b)).astype(_BF16)
    update = lax.dot_general(v, k_out, _TN, preferred_element_type=_F32)
    st_ref[slot] = state * jnp.exp2(b_exit) + update
    return scores, o_inter, v


def _scan_kernel(qf_ref, kf_ref, vf_ref, bf_ref, qb_ref, kb_ref, vb_ref, bb_ref, of_ref, ob_ref, st_ref):
    @pl.when(pl.program_id(1) == 0)
    def _():
        st_ref[...] = jnp.zeros_like(st_ref)

    t_idx = lax.broadcasted_iota(jnp.int32, (SCAN_CHUNK, SCAN_CHUNK), 0)
    s_idx = lax.broadcasted_iota(jnp.int32, (SCAN_CHUNK, SCAN_CHUNK), 1)
    streams = []
    for seq in range(SCAN_SEQS):
        streams.append((seq, False, (qf_ref, kf_ref, vf_ref, bf_ref), of_ref, s_idx <= t_idx))
        streams.append((seq, True, (qb_ref, kb_ref, vb_ref, bb_ref), ob_ref, s_idx >= t_idx))

    items = [(index, h) for index in range(len(streams)) for h in range(HEADS)]

    def stage1(index, h):
        seq, reverse, refs, _, _ = streams[index]
        return _scan_scores(*refs, st_ref, seq, h, index * HEADS + h, reverse)

    def stage2(index, h, scores, o_inter, v):
        seq, _, _, o_ref, seen = streams[index]
        o_ref[seq, :, h * HEAD_DIM:(h + 1) * HEAD_DIM] = (
            o_inter + _dot(jnp.where(seen, scores, 0.0).astype(_BF16), v))

    staged = {}
    for step in range(len(items) + SCAN_LAG):
        if step < len(items):
            staged[step] = stage1(*items[step])
        if step >= SCAN_LAG:
            stage2(*items[step - SCAN_LAG], *staged.pop(step - SCAN_LAG))


def _scan(q, kf, kb, v, bf, bb, *, batch):
    seq = q.shape[0] // batch
    nblocks = seq // SCAN_CHUNK
    shape3 = (batch, seq, D_MODEL)
    block = (SCAN_SEQS, SCAN_CHUNK, D_MODEL)
    fwd = pl.BlockSpec(block, lambda i, n: (i, n, 0))
    bwd = pl.BlockSpec(block, lambda i, n: (i, nblocks - 1 - n, 0))
    q, kf, kb, v, bf, bb = (a.reshape(shape3) for a in (q, kf, kb, v, bf, bb))
    out = jax.ShapeDtypeStruct(shape3, _F32)
    o_f, o_b = pl.pallas_call(
        _scan_kernel,
        out_shape=(out, out),
        grid=(batch // SCAN_SEQS, nblocks),
        in_specs=[fwd] * 4 + [bwd] * 4,
        out_specs=(fwd, bwd),
        scratch_shapes=[pltpu.VMEM((2 * SCAN_SEQS * HEADS, HEAD_DIM, HEAD_DIM), _F32)],
        compiler_params=_params("parallel", "arbitrary"),
        name="scan",
    )(q, kf, v, bf, q, kb, v, bb)
    return o_f.reshape(batch * seq, D_MODEL), o_b.reshape(batch * seq, D_MODEL)


def _trunk(x, w):
    batch, seq, _ = x.shape
    x = x.reshape(batch * seq, D_MODEL)
    depth = w["norm_w"].shape[0]
    for layer in range(depth):
        last = layer == depth - 1
        j = layer // 2
        x = _ffn(x, w["norm_w"][layer, 0][None], w["ffn_gate"][layer, 0], w["ffn_up"][layer, 0],
                 w["ffn_down"][layer, 0], w["final_norm"][None], final_norm=False)
        hgrn = None
        if layer % 2 == 0:
            x = _sgu(x, w["norm_w"][layer, 1][None], w["sgu_w_in"][j], w["sgu_ln_g"][j][None],
                     w["sgu_ln_b"][j][None], w["sgu_w_s"][j], w["sgu_b_s"][j], w["sgu_w_out"][j])
        else:
            q, kf, kb, v, gs, bf, bb = _hgrn_pre(x, w["norm_w"][layer, 1][None], w["hgrn_w_in"][j],
                                                 w["hgrn_lb_raw"][0], w["hgrn_lb_raw"][1], layer=layer)
            o_f, o_b = _scan(q, kf, kb, v, bf, bb, batch=batch)
            hgrn = (o_f, o_b, gs, w["hgrn_norm_w"][j][None], w["hgrn_w_out"][j])
        x = _ffn(x, w["norm_w"][layer, 2][None], w["ffn_gate"][layer, 1], w["ffn_up"][layer, 1],
                 w["ffn_down"][layer, 1], w["final_norm"][None], final_norm=last, hgrn=hgrn)
    return x.reshape(batch, seq, D_MODEL)


def kernel(x_prompt, x_sample, norm_w, ffn_gate, ffn_up, ffn_down, sgu_w_in, sgu_ln_g, sgu_ln_b, sgu_w_s,
           sgu_b_s, sgu_w_out, hgrn_w_in, hgrn_lb_raw, hgrn_norm_w, hgrn_w_out, final_norm):
    w = {
        "norm_w": norm_w,
        "ffn_gate": ffn_gate.astype(_BF16),
        "ffn_up": ffn_up.astype(_BF16),
        "ffn_down": ffn_down.astype(_BF16),
        "sgu_w_in": sgu_w_in.astype(_BF16),
        "sgu_ln_g": sgu_ln_g,
        "sgu_ln_b": sgu_ln_b,
        "sgu_w_s": sgu_w_s.astype(_BF16),
        "sgu_b_s": jnp.repeat(jnp.swapaxes(sgu_b_s, 1, 2), SGU_GROUP_DIM, axis=2),
        "sgu_w_out": sgu_w_out.astype(_BF16),
        "hgrn_w_in": hgrn_w_in.astype(_BF16),
        "hgrn_lb_raw": hgrn_lb_raw,
        "hgrn_norm_w": hgrn_norm_w,
        "hgrn_w_out": hgrn_w_out.astype(_BF16),
        "final_norm": final_norm,
    }
    return _trunk(x_prompt, w), _trunk(x_sample, w)
```

```python
import functools

import jax
import jax.numpy as jnp
from jax import lax
from jax.experimental import pallas as pl
from jax.experimental.pallas import tpu as pltpu

D_MODEL = 1024
FFN_DIM = 2816
SGU_DIM = 3 * D_MODEL
SGU_GROUPS = 8
SGU_GROUP_DIM = SGU_DIM // SGU_GROUPS
SGU_CHUNK = 128
HEAD_DIM = 128
HEADS = D_MODEL // HEAD_DIM
EPS = 1e-6

SCAN_CHUNK = 64
SUB_BLOCK = 16
SUB_REF = SUB_BLOCK // 2
SCORE_GROUP = 2
SCAN_SEQS = 4
SCAN_LAG = 5

FFN_ROWS = 512
FFN_PART = 256
FFN_HGRN_ROWS = 512
SGU_ROWS = 512
SGU_PART = 256
HGRN_ROWS = 512
HGRN_PART = 256

V7X_VMEM_LIMIT_BYTES = 56 * 1024 * 1024

_F32 = jnp.float32
_BF16 = jnp.bfloat16
_NT = (((1,), (1,)), ((), ()))
_TN = (((0,), (0,)), ((), ()))


def _params(*semantics):
    return pltpu.CompilerParams(dimension_semantics=semantics, vmem_limit_bytes=V7X_VMEM_LIMIT_BYTES)


def _row_spec(rows, cols):
    return pl.BlockSpec((rows, cols), lambda i: (i, 0))


def _resident_spec(shape):
    zeros = (0,) * len(shape)
    return pl.BlockSpec(shape, lambda *_: zeros, pipeline_mode=pl.Buffered(1))


def _rmsnorm(x, w):
    return x * lax.rsqrt(jnp.mean(x * x, axis=-1, keepdims=True) + EPS) * w


def _sigmoid(x):
    return 1.0 / (1.0 + jnp.exp(-x))


def _dot(a, b):
    return jnp.dot(a, b, preferred_element_type=_F32)


def _hgrn_mix(o, gate, nw, wout):
    parts = []
    for h in range(HEADS):
        oh = o[:, h * HEAD_DIM:(h + 1) * HEAD_DIM]
        parts.append(oh * lax.rsqrt(jnp.mean(oh * oh, axis=-1, keepdims=True) + EPS))
    o = jnp.concatenate(parts, axis=1) * nw * gate.astype(_F32)
    return _dot(o.astype(_BF16), wout)


def _ffn_kernel(*refs, final_norm, after_hgrn):
    if after_hgrn:
        of_ref, ob_ref, gs_ref, hnw_ref, hwo_ref, x_ref, nw_ref, wg_ref, wu_ref, wd_ref, fw_ref, o_ref = refs
    else:
        x_ref, nw_ref, wg_ref, wu_ref, wd_ref, fw_ref, o_ref = refs

    def residual(rows):
        x = x_ref[rows, :]
        if after_hgrn:
            x = x + _hgrn_mix(of_ref[rows, :] + ob_ref[rows, :], gs_ref[rows, :], hnw_ref[...], hwo_ref[...])
        return x

    def expand(x):
        h = _rmsnorm(x, nw_ref[...]).astype(_BF16)
        gate = _dot(h, wg_ref[...])
        up = _dot(h, wu_ref[...])
        return x, (gate * _sigmoid(gate) * up).astype(_BF16)

    def contract(rows, x, act):
        y = x + 0.5 * _dot(act, wd_ref[...])
        if final_norm:
            y = _rmsnorm(y, fw_ref[...])
        o_ref[rows, :] = y

    parts = [slice(n * FFN_PART, (n + 1) * FFN_PART) for n in range(x_ref.shape[0] // FFN_PART)]
    inputs = [residual(rows) for rows in parts]
    pending = expand(inputs[0])
    for previous, x in zip(parts, inputs[1:]):
        following = expand(x)
        contract(previous, *pending)
        pending = following
    contract(parts[-1], *pending)


def _ffn(x, nw, wg, wu, wd, fw, *, final_norm, hgrn=None):
    rows = x.shape[0]
    step = FFN_ROWS if hgrn is None else FFN_HGRN_ROWS
    spec = _row_spec(step, D_MODEL)
    hgrn_specs = [] if hgrn is None else [spec, spec, spec, _resident_spec((1, D_MODEL)),
                                          _resident_spec((D_MODEL, D_MODEL))]
    return pl.pallas_call(
        functools.partial(_ffn_kernel, final_norm=final_norm, after_hgrn=hgrn is not None),
        out_shape=jax.ShapeDtypeStruct(x.shape, _F32),
        grid=(rows // step,),
        in_specs=hgrn_specs + [
            spec,
            _resident_spec((1, D_MODEL)),
            _resident_spec((D_MODEL, FFN_DIM)),
            _resident_spec((D_MODEL, FFN_DIM)),
            _resident_spec((FFN_DIM, D_MODEL)),
            _resident_spec((1, D_MODEL)),
        ],
        out_specs=spec,
        compiler_params=_params("parallel"),
        name="ffn" if hgrn is None else "hgrn_out_ffn",
    )(*(hgrn or ()), x, nw, wg, wu, wd, fw)


def _sgu_kernel(x_ref, nw_ref, win_ref, lng_ref, lnb_ref, ws_ref, bs_ref, wout_ref, o_ref):
    def project(n):
        rows = slice(n * SGU_PART, (n + 1) * SGU_PART)
        h = _rmsnorm(x_ref[rows, :], nw_ref[...]).astype(_BF16)
        def gelu(z):
            return 0.5 * z * (1.0 + lax.erf(z * (2.0 ** -0.5)))

        v = gelu(_dot(h, win_ref[:, SGU_DIM:]))
        u = gelu(_dot(h, win_ref[:, :SGU_DIM]))
        mu = jnp.mean(v, axis=-1, keepdims=True)
        vc = v - mu
        v = vc * lax.rsqrt(jnp.mean(vc * vc, axis=-1, keepdims=True) + EPS) * lng_ref[...] + lnb_ref[...]
        return u.astype(_BF16), v.astype(_BF16)

    def mix(n, u, v):
        gated = []
        for c in range(SGU_PART // SGU_CHUNK):
            chunk = slice(c * SGU_CHUNK, (c + 1) * SGU_CHUNK)
            parts = []
            for g in range(SGU_GROUPS):
                cols = slice(g * SGU_GROUP_DIM, (g + 1) * SGU_GROUP_DIM)
                parts.append(_dot(ws_ref[g], v[chunk, cols]))
            s = jnp.concatenate(parts, axis=1) + bs_ref[...]
            gated.append((u[chunk].astype(_F32) * s).astype(_BF16))
        rows = slice(n * SGU_PART, (n + 1) * SGU_PART)
        o_ref[rows, :] = x_ref[rows, :] + _dot(jnp.concatenate(gated, axis=0), wout_ref[...])

    nparts = SGU_ROWS // SGU_PART
    pending = project(0)
    for n in range(1, nparts):
        following = project(n)
        mix(n - 1, *pending)
        pending = following
    mix(nparts - 1, *pending)


def _sgu(x, nw, win, lng, lnb, ws, bs, wout):
    rows = x.shape[0]
    return pl.pallas_call(
        _sgu_kernel,
        out_shape=jax.ShapeDtypeStruct(x.shape, _F32),
        grid=(rows // SGU_ROWS,),
        in_specs=[
            _row_spec(SGU_ROWS, D_MODEL),
            _resident_spec((1, D_MODEL)),
            _resident_spec((D_MODEL, 2 * SGU_DIM)),
            _resident_spec((1, SGU_DIM)),
            _resident_spec((1, SGU_DIM)),
            _resident_spec((SGU_GROUPS, SGU_CHUNK, SGU_CHUNK)),
            _resident_spec((SGU_CHUNK, SGU_DIM)),
            _resident_spec((SGU_DIM, D_MODEL)),
        ],
        out_specs=_row_spec(SGU_ROWS, D_MODEL),
        compiler_params=_params("parallel"),
        name="sgu",
    )(x, nw, win, lng, lnb, ws, bs, wout)


def _lower_bound(raw, layer):
    e = jnp.exp(raw - jnp.max(raw, axis=0, keepdims=True))
    p = e / jnp.sum(e, axis=0, keepdims=True)
    lb = jnp.zeros_like(p[0:1])
    for j in range(1, layer + 1):
        lb = lb + p[j:j + 1]
    return lb


CUMSUM_TERMS = 2


def _chunk_cumsum(ones_bf16, x):
    out = []
    for c in range(x.shape[0] // SCAN_CHUNK):
        rest = x[c * SCAN_CHUNK:(c + 1) * SCAN_CHUNK]
        terms = []
        for _ in range(CUMSUM_TERMS):
            terms.append(rest.astype(_BF16))
            rest = rest - terms[-1].astype(_F32)
        out.append(_dot(ones_bf16, jnp.concatenate(terms, axis=0)))
    return jnp.concatenate(out, axis=0)


def _hgrn_pre_kernel(x_ref, nw_ref, win_ref, lbf_ref, lbb_ref,
                     q_ref, kf_ref, kb_ref, v_ref, gs_ref, bf_ref, bb_ref, *, layer):
    row = lax.broadcasted_iota(jnp.int32, (SCAN_CHUNK, CUMSUM_TERMS * SCAN_CHUNK), 0)
    col = lax.broadcasted_iota(jnp.int32, (SCAN_CHUNK, CUMSUM_TERMS * SCAN_CHUNK), 1) % SCAN_CHUNK
    prefix = jnp.where(col <= row, 1.0, 0.0).astype(_BF16)
    suffix = jnp.where(col >= row, 1.0, 0.0).astype(_BF16)
    lb_f = _lower_bound(lbf_ref[...], layer)
    lb_b = _lower_bound(lbb_ref[...], layer)

    def project(rows):
        h = _rmsnorm(x_ref[rows, :], nw_ref[...]).astype(_BF16)

        def columns(part):
            return _dot(h, win_ref[:, part * D_MODEL:(part + 1) * D_MODEL])

        logit_f, logit_b = columns(1), columns(2)
        q = columns(0)
        q_ref[rows, :] = (q * _sigmoid(q)).astype(_BF16)
        v_ref[rows, :] = columns(3).astype(_BF16)
        g = columns(4)
        gs_ref[rows, :] = (g * _sigmoid(g)).astype(_BF16)
        f_f = lb_f + (1.0 - lb_f) * _sigmoid(logit_f)
        kf_ref[rows, :] = (1.0 - f_f).astype(_BF16)
        f_b = lb_b + (1.0 - lb_b) * _sigmoid(logit_b)
        kb_ref[rows, :] = (1.0 - f_b).astype(_BF16)
        return jnp.log2(f_f), jnp.log2(f_b)

    def accumulate(rows, log_f, log_b):
        bf_ref[rows, :] = _chunk_cumsum(prefix, log_f)
        bb_ref[rows, :] = _chunk_cumsum(suffix, log_b)

    for n in range(HGRN_ROWS // HGRN_PART):
        rows = slice(n * HGRN_PART, (n + 1) * HGRN_PART)
        accumulate(rows, *project(rows))


def _hgrn_pre(x, nw, win, lbf_raw, lbb_raw, *, layer):
    rows = x.shape[0]
    depth = lbf_raw.shape[0]
    half = jax.ShapeDtypeStruct(x.shape, _BF16)
    full = jax.ShapeDtypeStruct(x.shape, _F32)
    spec = _row_spec(HGRN_ROWS, D_MODEL)
    return pl.pallas_call(
        functools.partial(_hgrn_pre_kernel, layer=layer),
        out_shape=(half, half, half, half, half, full, full),
        grid=(rows // HGRN_ROWS,),
        in_specs=[
            spec,
            _resident_spec((1, D_MODEL)),
            _resident_spec((D_MODEL, 5 * D_MODEL)),
            _resident_spec((depth, D_MODEL)),
            _resident_spec((depth, D_MODEL)),
        ],
        out_specs=(spec,) * 7,
        compiler_params=_params("parallel"),
        name="hgrn_pre",
    )(x, nw, win, lbf_raw, lbb_raw)


def _scan_scores(q_ref, k_ref, v_ref, b_ref, st_ref, seq, head, slot, reverse):
    nsub = SCAN_CHUNK // SUB_BLOCK
    exit_row = 0 if reverse else SCAN_CHUNK - 1
    hs = slice(head * HEAD_DIM, (head + 1) * HEAD_DIM)
    b = b_ref[seq, :, hs]
    q = q_ref[seq, :, hs].astype(_F32)
    k = k_ref[seq, :, hs].astype(_F32)
    v = v_ref[seq, :, hs]
    b_exit = b[exit_row:exit_row + 1]
    state = st_ref[slot]
    q_in = (q * jnp.exp2(b)).astype(_BF16)
    o_inter = lax.dot_general(q_in, state.astype(_BF16), _NT, preferred_element_type=_F32)
    zero_block = jnp.zeros((SUB_BLOCK, HEAD_DIM), _BF16)
    score_rows = []
    for first in range(0, nsub, SCORE_GROUP):
        q_blocks, k_blocks = [], []
        for slot_in_group in range(SCORE_GROUP):
            blk = first + slot_in_group
            rs = slice(blk * SUB_BLOCK, (blk + 1) * SUB_BLOCK)
            cs = slice(blk * SUB_BLOCK, SCAN_CHUNK) if reverse else slice(0, (blk + 1) * SUB_BLOCK)
            b_mid = b[blk * SUB_BLOCK + SUB_REF:blk * SUB_BLOCK + SUB_REF + 1]
            q_row = [zero_block] * SCORE_GROUP
            q_row[slot_in_group] = (q[rs] * jnp.exp2(b[rs] - b_mid)).astype(_BF16)
            q_blocks.append(jnp.concatenate(q_row, axis=1))
            k_col = [zero_block] * nsub
            k_col[cs.start // SUB_BLOCK:cs.stop // SUB_BLOCK] = [(k[cs] * jnp.exp2(b_mid - b[cs])).astype(_BF16)]
            k_blocks.append(jnp.concatenate(k_col, axis=0))
        score_rows.append(lax.dot_general(jnp.concatenate(q_blocks, axis=0), jnp.concatenate(k_blocks, axis=1),
                                          _NT, preferred_element_type=_F32))
    scores = jnp.concatenate(score_rows, axis=0)
    k_out = (k * jnp.exp2(b_exit - b)).astype(_BF16)
    update = lax.dot_general(v, k_out, _TN, preferred_element_type=_F32)
    st_ref[slot] = state * jnp.exp2(b_exit) + update
    return scores, o_inter, v


def _scan_kernel(qf_ref, kf_ref, vf_ref, bf_ref, qb_ref, kb_ref, vb_ref, bb_ref, of_ref, ob_ref, st_ref):
    @pl.when(pl.program_id(1) == 0)
    def _():
        st_ref[...] = jnp.zeros_like(st_ref)

    t_idx = lax.broadcasted_iota(jnp.int32, (SCAN_CHUNK, SCAN_CHUNK), 0)
    s_idx = lax.broadcasted_iota(jnp.int32, (SCAN_CHUNK, SCAN_CHUNK), 1)
    streams = []
    for seq in range(SCAN_SEQS):
        streams.append((seq, False, (qf_ref, kf_ref, vf_ref, bf_ref), of_ref, s_idx <= t_idx))
        streams.append((seq, True, (qb_ref, kb_ref, vb_ref, bb_ref), ob_ref, s_idx >= t_idx))

    items = [(index, h) for index in range(len(streams)) for h in range(HEADS)]

    def stage1(index, h):
        seq, reverse, refs, _, _ = streams[index]
        return _scan_scores(*refs, st_ref, seq, h, index * HEADS + h, reverse)

    def stage2(index, h, scores, o_inter, v):
        seq, _, _, o_ref, seen = streams[index]
        o_ref[seq, :, h * HEAD_DIM:(h + 1) * HEAD_DIM] = (
            o_inter + _dot(jnp.where(seen, scores, 0.0).astype(_BF16), v))

    staged = {}
    for step in range(len(items) + SCAN_LAG):
        if step < len(items):
            staged[step] = stage1(*items[step])
        if step >= SCAN_LAG:
            stage2(*items[step - SCAN_LAG], *staged.pop(step - SCAN_LAG))


def _scan(q, kf, kb, v, bf, bb, *, batch):
    seq = q.shape[0] // batch
    nblocks = seq // SCAN_CHUNK
    shape3 = (batch, seq, D_MODEL)
    block = (SCAN_SEQS, SCAN_CHUNK, D_MODEL)
    fwd = pl.BlockSpec(block, lambda i, n: (i, n, 0))
    bwd = pl.BlockSpec(block, lambda i, n: (i, nblocks - 1 - n, 0))
    q, kf, kb, v, bf, bb = (a.reshape(shape3) for a in (q, kf, kb, v, bf, bb))
    out = jax.ShapeDtypeStruct(shape3, _F32)
    o_f, o_b = pl.pallas_call(
        _scan_kernel,
        out_shape=(out, out),
        grid=(batch // SCAN_SEQS, nblocks),
        in_specs=[fwd] * 4 + [bwd] * 4,
        out_specs=(fwd, bwd),
        scratch_shapes=[pltpu.VMEM((2 * SCAN_SEQS * HEADS, HEAD_DIM, HEAD_DIM), _F32)],
        compiler_params=_params("parallel", "arbitrary"),
        name="scan",
    )(q, kf, v, bf, q, kb, v, bb)
    return o_f.reshape(batch * seq, D_MODEL), o_b.reshape(batch * seq, D_MODEL)


def _trunk(x, w):
    batch, seq, _ = x.shape
    x = x.reshape(batch * seq, D_MODEL)
    depth = w["norm_w"].shape[0]
    for layer in range(depth):
        last = layer == depth - 1
        j = layer // 2
        x = _ffn(x, w["norm_w"][layer, 0][None], w["ffn_gate"][layer, 0], w["ffn_up"][layer, 0],
                 w["ffn_down"][layer, 0], w["final_norm"][None], final_norm=False)
        hgrn = None
        if layer % 2 == 0:
            x = _sgu(x, w["norm_w"][layer, 1][None], w["sgu_w_in"][j], w["sgu_ln_g"][j][None],
                     w["sgu_ln_b"][j][None], w["sgu_w_s"][j], w["sgu_b_s"][j], w["sgu_w_out"][j])
        else:
            q, kf, kb, v, gs, bf, bb = _hgrn_pre(x, w["norm_w"][layer, 1][None], w["hgrn_w_in"][j],
                                                 w["hgrn_lb_raw"][0], w["hgrn_lb_raw"][1], layer=layer)
            o_f, o_b = _scan(q, kf, kb, v, bf, bb, batch=batch)
            hgrn = (o_f, o_b, gs, w["hgrn_norm_w"][j][None], w["hgrn_w_out"][j])
        x = _ffn(x, w["norm_w"][layer, 2][None], w["ffn_gate"][layer, 1], w["ffn_up"][layer, 1],
                 w["ffn_down"][layer, 1], w["final_norm"][None], final_norm=last, hgrn=hgrn)
    return x.reshape(batch, seq, D_MODEL)


def kernel(x_prompt, x_sample, norm_w, ffn_gate, ffn_up, ffn_down, sgu_w_in, sgu_ln_g, sgu_ln_b, sgu_w_s,
           sgu_b_s, sgu_w_out, hgrn_w_in, hgrn_lb_raw, hgrn_norm_w, hgrn_w_out, final_norm):
    w = {
        "norm_w": norm_w,
        "ffn_gate": ffn_gate.astype(_BF16),
        "ffn_up": ffn_up.astype(_BF16),
        "ffn_down": ffn_down.astype(_BF16),
        "sgu_w_in": sgu_w_in.astype(_BF16),
        "sgu_ln_g": sgu_ln_g,
        "sgu_ln_b": sgu_ln_b,
        "sgu_w_s": sgu_w_s.astype(_BF16),
        "sgu_b_s": jnp.repeat(jnp.swapaxes(sgu_b_s, 1, 2), SGU_GROUP_DIM, axis=2),
        "sgu_w_out": sgu_w_out.astype(_BF16),
        "hgrn_w_in": hgrn_w_in.astype(_BF16),
        "hgrn_lb_raw": hgrn_lb_raw,
        "hgrn_norm_w": hgrn_norm_w,
        "hgrn_w_out": hgrn_w_out.astype(_BF16),
        "final_norm": final_norm,
    }
    return _trunk(x_prompt, w), _trunk(x_sample, w)
```

```python
import functools

import jax
import jax.numpy as jnp
from jax import lax
from jax.experimental import pallas as pl
from jax.experimental.pallas import tpu as pltpu

D_MODEL = 1024
FFN_DIM = 2816
SGU_DIM = 3 * D_MODEL
SGU_GROUPS = 8
SGU_GROUP_DIM = SGU_DIM // SGU_GROUPS
SGU_CHUNK = 128
HEAD_DIM = 128
HEADS = D_MODEL // HEAD_DIM
EPS = 1e-6

SCAN_CHUNK = 64
SUB_BLOCK = 16
SUB_REF = SUB_BLOCK // 2
SCORE_GROUP = 2
SCAN_SEQS = 4
SCAN_LAG = 5

FFN_ROWS = 1024
FFN_PARTS = 2
FFN_HGRN_ROWS = 512
SGU_ROWS = 512
SGU_PART = 256
HGRN_ROWS = 512
HGRN_PART = 256

V7X_VMEM_LIMIT_BYTES = 56 * 1024 * 1024

_F32 = jnp.float32
_BF16 = jnp.bfloat16
_NT = (((1,), (1,)), ((), ()))
_TN = (((0,), (0,)), ((), ()))


def _params(*semantics):
    return pltpu.CompilerParams(dimension_semantics=semantics, vmem_limit_bytes=V7X_VMEM_LIMIT_BYTES)


def _row_spec(rows, cols):
    return pl.BlockSpec((rows, cols), lambda i: (i, 0))


def _resident_spec(shape):
    zeros = (0,) * len(shape)
    return pl.BlockSpec(shape, lambda *_: zeros, pipeline_mode=pl.Buffered(1))


def _rmsnorm(x, w):
    return x * lax.rsqrt(jnp.mean(x * x, axis=-1, keepdims=True) + EPS) * w


def _sigmoid(x):
    return 1.0 / (1.0 + jnp.exp(-x))


def _dot(a, b):
    return jnp.dot(a, b, preferred_element_type=_F32)


def _hgrn_mix(o, gate, nw, wout):
    parts = []
    for h in range(HEADS):
        oh = o[:, h * HEAD_DIM:(h + 1) * HEAD_DIM]
        parts.append(oh * lax.rsqrt(jnp.mean(oh * oh, axis=-1, keepdims=True) + EPS))
    o = jnp.concatenate(parts, axis=1) * nw * gate.astype(_F32)
    return _dot(o.astype(_BF16), wout)


def _ffn_kernel(*refs, final_norm, after_hgrn):
    if after_hgrn:
        of_ref, ob_ref, gs_ref, hnw_ref, hwo_ref, x_ref, nw_ref, wg_ref, wu_ref, wd_ref, fw_ref, o_ref = refs
    else:
        x_ref, nw_ref, wg_ref, wu_ref, wd_ref, fw_ref, o_ref = refs

    def residual(rows):
        x = x_ref[rows, :]
        if after_hgrn:
            x = x + _hgrn_mix(of_ref[rows, :] + ob_ref[rows, :], gs_ref[rows, :], hnw_ref[...], hwo_ref[...])
        return x

    def expand(x):
        h = _rmsnorm(x, nw_ref[...]).astype(_BF16)
        gate = _dot(h, wg_ref[...])
        up = _dot(h, wu_ref[...])
        return x, (gate * _sigmoid(gate) * up).astype(_BF16)

    def contract(rows, x, act):
        y = x + 0.5 * _dot(act, wd_ref[...])
        if final_norm:
            y = _rmsnorm(y, fw_ref[...])
        o_ref[rows, :] = y

    part = x_ref.shape[0] // FFN_PARTS
    parts = [slice(n * part, (n + 1) * part) for n in range(FFN_PARTS)]
    inputs = [residual(rows) for rows in parts]
    pending = expand(inputs[0])
    for previous, x in zip(parts, inputs[1:]):
        following = expand(x)
        contract(previous, *pending)
        pending = following
    contract(parts[-1], *pending)


def _ffn(x, nw, wg, wu, wd, fw, *, final_norm, hgrn=None):
    rows = x.shape[0]
    step = FFN_ROWS if hgrn is None else FFN_HGRN_ROWS
    spec = _row_spec(step, D_MODEL)
    hgrn_specs = [] if hgrn is None else [spec, spec, spec, _resident_spec((1, D_MODEL)),
                                          _resident_spec((D_MODEL, D_MODEL))]
    return pl.pallas_call(
        functools.partial(_ffn_kernel, final_norm=final_norm, after_hgrn=hgrn is not None),
        out_shape=jax.ShapeDtypeStruct(x.shape, _F32),
        grid=(rows // step,),
        in_specs=hgrn_specs + [
            spec,
            _resident_spec((1, D_MODEL)),
            _resident_spec((D_MODEL, FFN_DIM)),
            _resident_spec((D_MODEL, FFN_DIM)),
            _resident_spec((FFN_DIM, D_MODEL)),
            _resident_spec((1, D_MODEL)),
        ],
        out_specs=spec,
        compiler_params=_params("parallel"),
        name="ffn" if hgrn is None else "hgrn_out_ffn",
    )(*(hgrn or ()), x, nw, wg, wu, wd, fw)


def _sgu_kernel(x_ref, nw_ref, win_ref, lng_ref, lnb_ref, ws_ref, bs_ref, wout_ref, o_ref):
    def project(n):
        rows = slice(n * SGU_PART, (n + 1) * SGU_PART)
        h = _rmsnorm(x_ref[rows, :], nw_ref[...]).astype(_BF16)
        def gelu(z):
            return 0.5 * z * (1.0 + lax.erf(z * (2.0 ** -0.5)))

        v = gelu(_dot(h, win_ref[:, SGU_DIM:]))
        u = gelu(_dot(h, win_ref[:, :SGU_DIM]))
        mu = jnp.mean(v, axis=-1, keepdims=True)
        vc = v - mu
        v = vc * lax.rsqrt(jnp.mean(vc * vc, axis=-1, keepdims=True) + EPS) * lng_ref[...] + lnb_ref[...]
        return u.astype(_BF16), v.astype(_BF16)

    def mix(n, u, v):
        gated = []
        for c in range(SGU_PART // SGU_CHUNK):
            chunk = slice(c * SGU_CHUNK, (c + 1) * SGU_CHUNK)
            parts = []
            for g in range(SGU_GROUPS):
                cols = slice(g * SGU_GROUP_DIM, (g + 1) * SGU_GROUP_DIM)
                parts.append(_dot(ws_ref[g], v[chunk, cols]))
            s = jnp.concatenate(parts, axis=1) + bs_ref[...]
            gated.append((u[chunk].astype(_F32) * s).astype(_BF16))
        rows = slice(n * SGU_PART, (n + 1) * SGU_PART)
        o_ref[rows, :] = x_ref[rows, :] + _dot(jnp.concatenate(gated, axis=0), wout_ref[...])

    nparts = SGU_ROWS // SGU_PART
    pending = project(0)
    for n in range(1, nparts):
        following = project(n)
        mix(n - 1, *pending)
        pending = following
    mix(nparts - 1, *pending)


def _sgu(x, nw, win, lng, lnb, ws, bs, wout):
    rows = x.shape[0]
    return pl.pallas_call(
        _sgu_kernel,
        out_shape=jax.ShapeDtypeStruct(x.shape, _F32),
        grid=(rows // SGU_ROWS,),
        in_specs=[
            _row_spec(SGU_ROWS, D_MODEL),
            _resident_spec((1, D_MODEL)),
            _resident_spec((D_MODEL, 2 * SGU_DIM)),
            _resident_spec((1, SGU_DIM)),
            _resident_spec((1, SGU_DIM)),
            _resident_spec((SGU_GROUPS, SGU_CHUNK, SGU_CHUNK)),
            _resident_spec((SGU_CHUNK, SGU_DIM)),
            _resident_spec((SGU_DIM, D_MODEL)),
        ],
        out_specs=_row_spec(SGU_ROWS, D_MODEL),
        compiler_params=_params("parallel"),
        name="sgu",
    )(x, nw, win, lng, lnb, ws, bs, wout)


def _lower_bound(raw, layer):
    e = jnp.exp(raw - jnp.max(raw, axis=0, keepdims=True))
    p = e / jnp.sum(e, axis=0, keepdims=True)
    lb = jnp.zeros_like(p[0:1])
    for j in range(1, layer + 1):
        lb = lb + p[j:j + 1]
    return lb


CUMSUM_TERMS = 2


def _chunk_cumsum(ones_bf16, x):
    out = []
    for c in range(x.shape[0] // SCAN_CHUNK):
        rest = x[c * SCAN_CHUNK:(c + 1) * SCAN_CHUNK]
        terms = []
        for _ in range(CUMSUM_TERMS):
            terms.append(rest.astype(_BF16))
            rest = rest - terms[-1].astype(_F32)
        out.append(_dot(ones_bf16, jnp.concatenate(terms, axis=0)))
    return jnp.concatenate(out, axis=0)


def _hgrn_pre_kernel(x_ref, nw_ref, win_ref, lbf_ref, lbb_ref,
                     q_ref, kf_ref, kb_ref, v_ref, gs_ref, bf_ref, bb_ref, *, layer):
    row = lax.broadcasted_iota(jnp.int32, (SCAN_CHUNK, CUMSUM_TERMS * SCAN_CHUNK), 0)
    col = lax.broadcasted_iota(jnp.int32, (SCAN_CHUNK, CUMSUM_TERMS * SCAN_CHUNK), 1) % SCAN_CHUNK
    prefix = jnp.where(col <= row, 1.0, 0.0).astype(_BF16)
    suffix = jnp.where(col >= row, 1.0, 0.0).astype(_BF16)
    lb_f = _lower_bound(lbf_ref[...], layer)
    lb_b = _lower_bound(lbb_ref[...], layer)

    def project(rows):
        h = _rmsnorm(x_ref[rows, :], nw_ref[...]).astype(_BF16)

        def columns(part):
            return _dot(h, win_ref[:, part * D_MODEL:(part + 1) * D_MODEL])

        logit_f, logit_b = columns(1), columns(2)
        q = columns(0)
        q_ref[rows, :] = (q * _sigmoid(q)).astype(_BF16)
        v_ref[rows, :] = columns(3).astype(_BF16)
        g = columns(4)
        gs_ref[rows, :] = (g * _sigmoid(g)).astype(_BF16)
        f_f = lb_f + (1.0 - lb_f) * _sigmoid(logit_f)
        kf_ref[rows, :] = (1.0 - f_f).astype(_BF16)
        f_b = lb_b + (1.0 - lb_b) * _sigmoid(logit_b)
        kb_ref[rows, :] = (1.0 - f_b).astype(_BF16)
        return jnp.log2(f_f), jnp.log2(f_b)

    def accumulate(rows, log_f, log_b):
        bf_ref[rows, :] = _chunk_cumsum(prefix, log_f)
        bb_ref[rows, :] = _chunk_cumsum(suffix, log_b)

    for n in range(HGRN_ROWS // HGRN_PART):
        rows = slice(n * HGRN_PART, (n + 1) * HGRN_PART)
        accumulate(rows, *project(rows))


def _hgrn_pre(x, nw, win, lbf_raw, lbb_raw, *, layer):
    rows = x.shape[0]
    depth = lbf_raw.shape[0]
    half = jax.ShapeDtypeStruct(x.shape, _BF16)
    full = jax.ShapeDtypeStruct(x.shape, _F32)
    spec = _row_spec(HGRN_ROWS, D_MODEL)
    return pl.pallas_call(
        functools.partial(_hgrn_pre_kernel, layer=layer),
        out_shape=(half, half, half, half, half, full, full),
        grid=(rows // HGRN_ROWS,),
        in_specs=[
            spec,
            _resident_spec((1, D_MODEL)),
            _resident_spec((D_MODEL, 5 * D_MODEL)),
            _resident_spec((depth, D_MODEL)),
            _resident_spec((depth, D_MODEL)),
        ],
        out_specs=(spec,) * 7,
        compiler_params=_params("parallel"),
        name="hgrn_pre",
    )(x, nw, win, lbf_raw, lbb_raw)


def _scan_scores(q_ref, k_ref, v_ref, b_ref, st_ref, seq, head, slot, reverse):
    nsub = SCAN_CHUNK // SUB_BLOCK
    exit_row = 0 if reverse else SCAN_CHUNK - 1
    hs = slice(head * HEAD_DIM, (head + 1) * HEAD_DIM)
    b = b_ref[seq, :, hs]
    q = q_ref[seq, :, hs].astype(_F32)
    k = k_ref[seq, :, hs].astype(_F32)
    v = v_ref[seq, :, hs]
    b_exit = b[exit_row:exit_row + 1]
    state = st_ref[slot]
    q_in = (q * jnp.exp2(b)).astype(_BF16)
    o_inter = lax.dot_general(q_in, state.astype(_BF16), _NT, preferred_element_type=_F32)
    zero_block = jnp.zeros((SUB_BLOCK, HEAD_DIM), _BF16)
    score_rows = []
    for first in range(0, nsub, SCORE_GROUP):
        q_blocks, k_blocks = [], []
        for slot_in_group in range(SCORE_GROUP):
            blk = first + slot_in_group
            rs = slice(blk * SUB_BLOCK, (blk + 1) * SUB_BLOCK)
            cs = slice(blk * SUB_BLOCK, SCAN_CHUNK) if reverse else slice(0, (blk + 1) * SUB_BLOCK)
            b_mid = b[blk * SUB_BLOCK + SUB_REF:blk * SUB_BLOCK + SUB_REF + 1]
            q_row = [zero_block] * SCORE_GROUP
            q_row[slot_in_group] = (q[rs] * jnp.exp2(b[rs] - b_mid)).astype(_BF16)
            q_blocks.append(jnp.concatenate(q_row, axis=1))
            k_col = [zero_block] * nsub
            k_col[cs.start // SUB_BLOCK:cs.stop // SUB_BLOCK] = [(k[cs] * jnp.exp2(b_mid - b[cs])).astype(_BF16)]
            k_blocks.append(jnp.concatenate(k_col, axis=0))
        score_rows.append(lax.dot_general(jnp.concatenate(q_blocks, axis=0), jnp.concatenate(k_blocks, axis=1),
                                          _NT, preferred_element_type=_F32))
    scores = jnp.concatenate(score_rows, axis=0)
    k_out = (k * jnp.exp2(b_exit - b)).astype(_BF16)
    update = lax.dot_general(v, k_out, _TN, preferred_element_type=_F32)
    st_ref[slot] = state * jnp.exp2(b_exit) + update
    return scores, o_inter, v


def _scan_kernel(qf_ref, kf_ref, vf_ref, bf_ref, qb_ref, kb_ref, vb_ref, bb_ref, of_ref, ob_ref, st_ref):
    @pl.when(pl.program_id(1) == 0)
    def _():
        st_ref[...] = jnp.zeros_like(st_ref)

    t_idx = lax.broadcasted_iota(jnp.int32, (SCAN_CHUNK, SCAN_CHUNK), 0)
    s_idx = lax.broadcasted_iota(jnp.int32, (SCAN_CHUNK, SCAN_CHUNK), 1)
    streams = []
    for seq in range(SCAN_SEQS):
        streams.append((seq, False, (qf_ref, kf_ref, vf_ref, bf_ref), of_ref, s_idx <= t_idx))
        streams.append((seq, True, (qb_ref, kb_ref, vb_ref, bb_ref), ob_ref, s_idx >= t_idx))

    items = [(index, h) for index in range(len(streams)) for h in range(HEADS)]

    def stage1(index, h):
        seq, reverse, refs, _, _ = streams[index]
        return _scan_scores(*refs, st_ref, seq, h, index * HEADS + h, reverse)

    def stage2(index, h, scores, o_inter, v):
        seq, _, _, o_ref, seen = streams[index]
        o_ref[seq, :, h * HEAD_DIM:(h + 1) * HEAD_DIM] = (
            o_inter + _dot(jnp.where(seen, scores, 0.0).astype(_BF16), v))

    staged = {}
    for step in range(len(items) + SCAN_LAG):
        if step < len(items):
            staged[step] = stage1(*items[step])
        if step >= SCAN_LAG:
            stage2(*items[step - SCAN_LAG], *staged.pop(step - SCAN_LAG))


def _scan(q, kf, kb, v, bf, bb, *, batch):
    seq = q.shape[0] // batch
    nblocks = seq // SCAN_CHUNK
    shape3 = (batch, seq, D_MODEL)
    block = (SCAN_SEQS, SCAN_CHUNK, D_MODEL)
    fwd = pl.BlockSpec(block, lambda i, n: (i, n, 0))
    bwd = pl.BlockSpec(block, lambda i, n: (i, nblocks - 1 - n, 0))
    q, kf, kb, v, bf, bb = (a.reshape(shape3) for a in (q, kf, kb, v, bf, bb))
    out = jax.ShapeDtypeStruct(shape3, _F32)
    o_f, o_b = pl.pallas_call(
        _scan_kernel,
        out_shape=(out, out),
        grid=(batch // SCAN_SEQS, nblocks),
        in_specs=[fwd] * 4 + [bwd] * 4,
        out_specs=(fwd, bwd),
        scratch_shapes=[pltpu.VMEM((2 * SCAN_SEQS * HEADS, HEAD_DIM, HEAD_DIM), _F32)],
        compiler_params=_params("parallel", "arbitrary"),
        name="scan",
    )(q, kf, v, bf, q, kb, v, bb)
    return o_f.reshape(batch * seq, D_MODEL), o_b.reshape(batch * seq, D_MODEL)


def _trunk(x, w):
    batch, seq, _ = x.shape
    x = x.reshape(batch * seq, D_MODEL)
    depth = w["norm_w"].shape[0]
    for layer in range(depth):
        last = layer == depth - 1
        j = layer // 2
        x = _ffn(x, w["norm_w"][layer, 0][None], w["ffn_gate"][layer, 0], w["ffn_up"][layer, 0],
                 w["ffn_down"][layer, 0], w["final_norm"][None], final_norm=False)
        hgrn = None
        if layer % 2 == 0:
            x = _sgu(x, w["norm_w"][layer, 1][None], w["sgu_w_in"][j], w["sgu_ln_g"][j][None],
                     w["sgu_ln_b"][j][None], w["sgu_w_s"][j], w["sgu_b_s"][j], w["sgu_w_out"][j])
        else:
            q, kf, kb, v, gs, bf, bb = _hgrn_pre(x, w["norm_w"][layer, 1][None], w["hgrn_w_in"][j],
                                                 w["hgrn_lb_raw"][0], w["hgrn_lb_raw"][1], layer=layer)
            o_f, o_b = _scan(q, kf, kb, v, bf, bb, batch=batch)
            hgrn = (o_f, o_b, gs, w["hgrn_norm_w"][j][None], w["hgrn_w_out"][j])
        x = _ffn(x, w["norm_w"][layer, 2][None], w["ffn_gate"][layer, 1], w["ffn_up"][layer, 1],
                 w["ffn_down"][layer, 1], w["final_norm"][None], final_norm=last, hgrn=hgrn)
    return x.reshape(batch, seq, D_MODEL)


def kernel(x_prompt, x_sample, norm_w, ffn_gate, ffn_up, ffn_down, sgu_w_in, sgu_ln_g, sgu_ln_b, sgu_w_s,
           sgu_b_s, sgu_w_out, hgrn_w_in, hgrn_lb_raw, hgrn_norm_w, hgrn_w_out, final_norm):
    w = {
        "norm_w": norm_w,
        "ffn_gate": ffn_gate.astype(_BF16),
        "ffn_up": ffn_up.astype(_BF16),
        "ffn_down": ffn_down.astype(_BF16),
        "sgu_w_in": sgu_w_in.astype(_BF16),
        "sgu_ln_g": sgu_ln_g,
        "sgu_ln_b": sgu_ln_b,
        "sgu_w_s": sgu_w_s.astype(_BF16),
        "sgu_b_s": jnp.repeat(jnp.swapaxes(sgu_b_s, 1, 2), SGU_GROUP_DIM, axis=2),
        "sgu_w_out": sgu_w_out.astype(_BF16),
        "hgrn_w_in": hgrn_w_in.astype(_BF16),
        "hgrn_lb_raw": hgrn_lb_raw,
        "hgrn_norm_w": hgrn_norm_w,
        "hgrn_w_out": hgrn_w_out.astype(_BF16),
        "final_norm": final_norm,
    }
    return _trunk(x_prompt, w), _trunk(x_sample, w)
```

```python
import functools

import jax
import jax.numpy as jnp
from jax import lax
from jax.experimental import pallas as pl
from jax.experimental.pallas import tpu as pltpu

D_MODEL = 1024
FFN_DIM = 2816
SGU_DIM = 3 * D_MODEL
SGU_GROUPS = 8
SGU_GROUP_DIM = SGU_DIM // SGU_GROUPS
SGU_CHUNK = 128
HEAD_DIM = 128
HEADS = D_MODEL // HEAD_DIM
EPS = 1e-6

SCAN_CHUNK = 64
SUB_BLOCK = 16
SUB_REF = SUB_BLOCK // 2
SCORE_GROUP = 2
SCAN_SEQS = 4
SCAN_LAG = 5

FFN_ROWS = 512
FFN_PARTS = 2
FFN_HGRN_ROWS = 512
SGU_ROWS = 512
SGU_PART = 256
HGRN_ROWS = 512
HGRN_PART = 256

V7X_VMEM_LIMIT_BYTES = 56 * 1024 * 1024

_F32 = jnp.float32
_BF16 = jnp.bfloat16
_NT = (((1,), (1,)), ((), ()))
_TN = (((0,), (0,)), ((), ()))


def _params(*semantics):
    return pltpu.CompilerParams(dimension_semantics=semantics, vmem_limit_bytes=V7X_VMEM_LIMIT_BYTES)


def _row_spec(rows, cols):
    return pl.BlockSpec((rows, cols), lambda i: (i, 0))


def _resident_spec(shape):
    zeros = (0,) * len(shape)
    return pl.BlockSpec(shape, lambda *_: zeros, pipeline_mode=pl.Buffered(1))


def _rmsnorm(x, w):
    return x * lax.rsqrt(jnp.mean(x * x, axis=-1, keepdims=True) + EPS) * w


def _sigmoid(x):
    return 1.0 / (1.0 + jnp.exp(-x))


def _dot(a, b):
    return jnp.dot(a, b, preferred_element_type=_F32)


def _hgrn_mix(o, gate, nw, wout):
    parts = []
    for h in range(HEADS):
        oh = o[:, h * HEAD_DIM:(h + 1) * HEAD_DIM]
        parts.append(oh * lax.rsqrt(jnp.mean(oh * oh, axis=-1, keepdims=True) + EPS))
    o = jnp.concatenate(parts, axis=1) * nw * gate.astype(_F32)
    return _dot(o.astype(_BF16), wout)


def _ffn_kernel(*refs, final_norm, after_hgrn):
    if after_hgrn:
        of_ref, ob_ref, gs_ref, hnw_ref, hwo_ref, x_ref, nw_ref, wg_ref, wu_ref, wd_ref, fw_ref, o_ref = refs
    else:
        x_ref, nw_ref, wg_ref, wu_ref, wd_ref, fw_ref, o_ref = refs

    def residual(rows):
        x = x_ref[rows, :]
        if after_hgrn:
            x = x + _hgrn_mix(of_ref[rows, :] + ob_ref[rows, :], gs_ref[rows, :], hnw_ref[...], hwo_ref[...])
        return x

    def expand(x):
        h = _rmsnorm(x, nw_ref[...]).astype(_BF16)
        gate = _dot(h, wg_ref[...])
        up = _dot(h, wu_ref[...])
        return x, (gate * _sigmoid(gate) * up).astype(_BF16)

    def contract(rows, x, act):
        y = x + 0.5 * _dot(act, wd_ref[...])
        if final_norm:
            y = _rmsnorm(y, fw_ref[...])
        o_ref[rows, :] = y

    part = x_ref.shape[0] // FFN_PARTS
    parts = [slice(n * part, (n + 1) * part) for n in range(FFN_PARTS)]
    inputs = [residual(rows) for rows in parts]
    pending = expand(inputs[0])
    for previous, x in zip(parts, inputs[1:]):
        following = expand(x)
        contract(previous, *pending)
        pending = following
    contract(parts[-1], *pending)


def _ffn(x, nw, wg, wu, wd, fw, *, final_norm, hgrn=None):
    rows = x.shape[0]
    step = FFN_ROWS if hgrn is None else FFN_HGRN_ROWS
    spec = _row_spec(step, D_MODEL)
    hgrn_specs = [] if hgrn is None else [spec, spec, spec, _resident_spec((1, D_MODEL)),
                                          _resident_spec((D_MODEL, D_MODEL))]
    return pl.pallas_call(
        functools.partial(_ffn_kernel, final_norm=final_norm, after_hgrn=hgrn is not None),
        out_shape=jax.ShapeDtypeStruct(x.shape, _F32),
        grid=(rows // step,),
        in_specs=hgrn_specs + [
            spec,
            _resident_spec((1, D_MODEL)),
            _resident_spec((D_MODEL, FFN_DIM)),
            _resident_spec((D_MODEL, FFN_DIM)),
            _resident_spec((FFN_DIM, D_MODEL)),
            _resident_spec((1, D_MODEL)),
        ],
        out_specs=spec,
        compiler_params=_params("parallel"),
        name="ffn" if hgrn is None else "hgrn_out_ffn",
    )(*(hgrn or ()), x, nw, wg, wu, wd, fw)


def _sgu_kernel(x_ref, nw_ref, win_ref, lng_ref, lnb_ref, ws_ref, bs_ref, wout_ref, o_ref):
    def project(n):
        rows = slice(n * SGU_PART, (n + 1) * SGU_PART)
        h = _rmsnorm(x_ref[rows, :], nw_ref[...]).astype(_BF16)
        def gelu(z):
            return 0.5 * z * (1.0 + lax.erf(z * (2.0 ** -0.5)))

        v = gelu(_dot(h, win_ref[:, SGU_DIM:]))
        u = gelu(_dot(h, win_ref[:, :SGU_DIM]))
        mu = jnp.mean(v, axis=-1, keepdims=True)
        vc = v - mu
        v = vc * lax.rsqrt(jnp.mean(vc * vc, axis=-1, keepdims=True) + EPS) * lng_ref[...] + lnb_ref[...]
        return u.astype(_BF16), v.astype(_BF16)

    def mix(n, u, v):
        gated = []
        for c in range(SGU_PART // SGU_CHUNK):
            chunk = slice(c * SGU_CHUNK, (c + 1) * SGU_CHUNK)
            parts = []
            for g in range(SGU_GROUPS):
                cols = slice(g * SGU_GROUP_DIM, (g + 1) * SGU_GROUP_DIM)
                parts.append(_dot(ws_ref[g], v[chunk, cols]))
            s = jnp.concatenate(parts, axis=1) + bs_ref[...]
            gated.append((u[chunk].astype(_F32) * s).astype(_BF16))
        rows = slice(n * SGU_PART, (n + 1) * SGU_PART)
        o_ref[rows, :] = x_ref[rows, :] + _dot(jnp.concatenate(gated, axis=0), wout_ref[...])

    nparts = SGU_ROWS // SGU_PART
    pending = project(0)
    for n in range(1, nparts):
        following = project(n)
        mix(n - 1, *pending)
        pending = following
    mix(nparts - 1, *pending)


def _sgu(x, nw, win, lng, lnb, ws, bs, wout):
    rows = x.shape[0]
    return pl.pallas_call(
        _sgu_kernel,
        out_shape=jax.ShapeDtypeStruct(x.shape, _F32),
        grid=(rows // SGU_ROWS,),
        in_specs=[
            _row_spec(SGU_ROWS, D_MODEL),
            _resident_spec((1, D_MODEL)),
            _resident_spec((D_MODEL, 2 * SGU_DIM)),
            _resident_spec((1, SGU_DIM)),
            _resident_spec((1, SGU_DIM)),
            _resident_spec((SGU_GROUPS, SGU_CHUNK, SGU_CHUNK)),
            _resident_spec((SGU_CHUNK, SGU_DIM)),
            _resident_spec((SGU_DIM, D_MODEL)),
        ],
        out_specs=_row_spec(SGU_ROWS, D_MODEL),
        compiler_params=_params("parallel"),
        name="sgu",
    )(x, nw, win, lng, lnb, ws, bs, wout)


def _lower_bound(raw, layer):
    e = jnp.exp(raw - jnp.max(raw, axis=0, keepdims=True))
    p = e / jnp.sum(e, axis=0, keepdims=True)
    lb = jnp.zeros_like(p[0:1])
    for j in range(1, layer + 1):
        lb = lb + p[j:j + 1]
    return lb


CUMSUM_TERMS = 2


def _chunk_cumsum(ones_bf16, x):
    out = []
    for c in range(x.shape[0] // SCAN_CHUNK):
        rest = x[c * SCAN_CHUNK:(c + 1) * SCAN_CHUNK]
        terms = []
        for _ in range(CUMSUM_TERMS):
            terms.append(rest.astype(_BF16))
            rest = rest - terms[-1].astype(_F32)
        out.append(_dot(ones_bf16, jnp.concatenate(terms, axis=0)))
    return jnp.concatenate(out, axis=0)


def _hgrn_pre_kernel(x_ref, nw_ref, win_ref, lbf_ref, lbb_ref,
                     q_ref, kf_ref, kb_ref, v_ref, gs_ref, bf_ref, bb_ref, *, layer):
    row = lax.broadcasted_iota(jnp.int32, (SCAN_CHUNK, CUMSUM_TERMS * SCAN_CHUNK), 0)
    col = lax.broadcasted_iota(jnp.int32, (SCAN_CHUNK, CUMSUM_TERMS * SCAN_CHUNK), 1) % SCAN_CHUNK
    prefix = jnp.where(col <= row, 1.0, 0.0).astype(_BF16)
    suffix = jnp.where(col >= row, 1.0, 0.0).astype(_BF16)
    lb_f = _lower_bound(lbf_ref[...], layer)
    lb_b = _lower_bound(lbb_ref[...], layer)

    def project(rows):
        h = _rmsnorm(x_ref[rows, :], nw_ref[...]).astype(_BF16)

        def columns(part):
            return _dot(h, win_ref[:, part * D_MODEL:(part + 1) * D_MODEL])

        logit_f, logit_b = columns(1), columns(2)
        q = columns(0)
        q_ref[rows, :] = (q * _sigmoid(q)).astype(_BF16)
        v_ref[rows, :] = columns(3).astype(_BF16)
        g = columns(4)
        gs_ref[rows, :] = (g * _sigmoid(g)).astype(_BF16)
        f_f = lb_f + (1.0 - lb_f) * _sigmoid(logit_f)
        kf_ref[rows, :] = (1.0 - f_f).astype(_BF16)
        f_b = lb_b + (1.0 - lb_b) * _sigmoid(logit_b)
        kb_ref[rows, :] = (1.0 - f_b).astype(_BF16)
        return jnp.log2(f_f), jnp.log2(f_b)

    def accumulate(rows, log_f, log_b):
        bf_ref[rows, :] = _chunk_cumsum(prefix, log_f)
        bb_ref[rows, :] = _chunk_cumsum(suffix, log_b)

    for n in range(HGRN_ROWS // HGRN_PART):
        rows = slice(n * HGRN_PART, (n + 1) * HGRN_PART)
        accumulate(rows, *project(rows))


def _hgrn_pre(x, nw, win, lbf_raw, lbb_raw, *, layer):
    rows = x.shape[0]
    depth = lbf_raw.shape[0]
    half = jax.ShapeDtypeStruct(x.shape, _BF16)
    full = jax.ShapeDtypeStruct(x.shape, _F32)
    spec = _row_spec(HGRN_ROWS, D_MODEL)
    return pl.pallas_call(
        functools.partial(_hgrn_pre_kernel, layer=layer),
        out_shape=(half, half, half, half, half, full, full),
        grid=(rows // HGRN_ROWS,),
        in_specs=[
            spec,
            _resident_spec((1, D_MODEL)),
            _resident_spec((D_MODEL, 5 * D_MODEL)),
            _resident_spec((depth, D_MODEL)),
            _resident_spec((depth, D_MODEL)),
        ],
        out_specs=(spec,) * 7,
        compiler_params=_params("parallel"),
        name="hgrn_pre",
    )(x, nw, win, lbf_raw, lbb_raw)


def _scan_scores(q_ref, k_ref, v_ref, b_ref, st_ref, seq, head, slot, reverse):
    nsub = SCAN_CHUNK // SUB_BLOCK
    exit_row = 0 if reverse else SCAN_CHUNK - 1
    hs = slice(head * HEAD_DIM, (head + 1) * HEAD_DIM)
    b = b_ref[seq, :, hs]
    q = q_ref[seq, :, hs].astype(_F32)
    k = k_ref[seq, :, hs].astype(_F32)
    v = v_ref[seq, :, hs]
    b_exit = b[exit_row:exit_row + 1]
    state = st_ref[slot]
    q_in = (q * jnp.exp2(b)).astype(_BF16)
    o_inter = _dot(q_in, state.astype(_BF16))
    zero_block = jnp.zeros((SUB_BLOCK, HEAD_DIM), _BF16)
    score_rows = []
    for first in range(0, nsub, SCORE_GROUP):
        q_blocks, k_blocks = [], []
        for slot_in_group in range(SCORE_GROUP):
            blk = first + slot_in_group
            rs = slice(blk * SUB_BLOCK, (blk + 1) * SUB_BLOCK)
            cs = slice(blk * SUB_BLOCK, SCAN_CHUNK) if reverse else slice(0, (blk + 1) * SUB_BLOCK)
            b_mid = b[blk * SUB_BLOCK + SUB_REF:blk * SUB_BLOCK + SUB_REF + 1]
            q_row = [zero_block] * SCORE_GROUP
            q_row[slot_in_group] = (q[rs] * jnp.exp2(b[rs] - b_mid)).astype(_BF16)
            q_blocks.append(jnp.concatenate(q_row, axis=1))
            k_col = [zero_block] * nsub
            k_col[cs.start // SUB_BLOCK:cs.stop // SUB_BLOCK] = [(k[cs] * jnp.exp2(b_mid - b[cs])).astype(_BF16)]
            k_blocks.append(jnp.concatenate(k_col, axis=0))
        score_rows.append(lax.dot_general(jnp.concatenate(q_blocks, axis=0), jnp.concatenate(k_blocks, axis=1),
                                          _NT, preferred_element_type=_F32))
    scores = jnp.concatenate(score_rows, axis=0)
    k_out = (k * jnp.exp2(b_exit - b)).astype(_BF16)
    update = lax.dot_general(k_out, v, _TN, preferred_element_type=_F32)
    decay = jnp.transpose(jnp.broadcast_to(jnp.exp2(b_exit), (HEAD_DIM, HEAD_DIM)))
    st_ref[slot] = state * decay + update
    return scores, o_inter, v


def _scan_kernel(qf_ref, kf_ref, vf_ref, bf_ref, qb_ref, kb_ref, vb_ref, bb_ref, of_ref, ob_ref, st_ref):
    @pl.when(pl.program_id(1) == 0)
    def _():
        st_ref[...] = jnp.zeros_like(st_ref)

    t_idx = lax.broadcasted_iota(jnp.int32, (SCAN_CHUNK, SCAN_CHUNK), 0)
    s_idx = lax.broadcasted_iota(jnp.int32, (SCAN_CHUNK, SCAN_CHUNK), 1)
    streams = []
    for seq in range(SCAN_SEQS):
        streams.append((seq, False, (qf_ref, kf_ref, vf_ref, bf_ref), of_ref, s_idx <= t_idx))
        streams.append((seq, True, (qb_ref, kb_ref, vb_ref, bb_ref), ob_ref, s_idx >= t_idx))

    items = [(index, h) for index in range(len(streams)) for h in range(HEADS)]

    def stage1(index, h):
        seq, reverse, refs, _, _ = streams[index]
        return _scan_scores(*refs, st_ref, seq, h, index * HEADS + h, reverse)

    def stage2(index, h, scores, o_inter, v):
        seq, _, _, o_ref, seen = streams[index]
        o_ref[seq, :, h * HEAD_DIM:(h + 1) * HEAD_DIM] = (
            o_inter + _dot(jnp.where(seen, scores, 0.0).astype(_BF16), v))

    staged = {}
    for step in range(len(items) + SCAN_LAG):
        if step < len(items):
            staged[step] = stage1(*items[step])
        if step >= SCAN_LAG:
            stage2(*items[step - SCAN_LAG], *staged.pop(step - SCAN_LAG))


def _scan(q, kf, kb, v, bf, bb, *, batch):
    seq = q.shape[0] // batch
    assert batch % SCAN_SEQS == 0 and seq % SCAN_CHUNK == 0, (batch, seq)
    nblocks = seq // SCAN_CHUNK
    shape3 = (batch, seq, D_MODEL)
    block = (SCAN_SEQS, SCAN_CHUNK, D_MODEL)
    fwd = pl.BlockSpec(block, lambda i, n: (i, n, 0))
    bwd = pl.BlockSpec(block, lambda i, n: (i, nblocks - 1 - n, 0))
    q, kf, kb, v, bf, bb = (a.reshape(shape3) for a in (q, kf, kb, v, bf, bb))
    out = jax.ShapeDtypeStruct(shape3, _F32)
    o_f, o_b = pl.pallas_call(
        _scan_kernel,
        out_shape=(out, out),
        grid=(batch // SCAN_SEQS, nblocks),
        in_specs=[fwd] * 4 + [bwd] * 4,
        out_specs=(fwd, bwd),
        scratch_shapes=[pltpu.VMEM((2 * SCAN_SEQS * HEADS, HEAD_DIM, HEAD_DIM), _F32)],
        compiler_params=_params("parallel", "arbitrary"),
        name="scan",
    )(q, kf, v, bf, q, kb, v, bb)
    return o_f.reshape(batch * seq, D_MODEL), o_b.reshape(batch * seq, D_MODEL)


def _trunk(x, w):
    batch, seq, _ = x.shape
    assert (batch * seq) % max(FFN_ROWS, FFN_HGRN_ROWS, SGU_ROWS, HGRN_ROWS) == 0 and seq % SGU_CHUNK == 0, x.shape
    x = x.reshape(batch * seq, D_MODEL)
    depth = w["norm_w"].shape[0]
    for layer in range(depth):
        last = layer == depth - 1
        j = layer // 2
        x = _ffn(x, w["norm_w"][layer, 0][None], w["ffn_gate"][layer, 0], w["ffn_up"][layer, 0],
                 w["ffn_down"][layer, 0], w["final_norm"][None], final_norm=False)
        hgrn = None
        if layer % 2 == 0:
            x = _sgu(x, w["norm_w"][layer, 1][None], w["sgu_w_in"][j], w["sgu_ln_g"][j][None],
                     w["sgu_ln_b"][j][None], w["sgu_w_s"][j], w["sgu_b_s"][j], w["sgu_w_out"][j])
        else:
            q, kf, kb, v, gs, bf, bb = _hgrn_pre(x, w["norm_w"][layer, 1][None], w["hgrn_w_in"][j],
                                                 w["hgrn_lb_raw"][0], w["hgrn_lb_raw"][1], layer=layer)
            o_f, o_b = _scan(q, kf, kb, v, bf, bb, batch=batch)
            hgrn = (o_f, o_b, gs, w["hgrn_norm_w"][j][None], w["hgrn_w_out"][j])
        x = _ffn(x, w["norm_w"][layer, 2][None], w["ffn_gate"][layer, 1], w["ffn_up"][layer, 1],
                 w["ffn_down"][layer, 1], w["final_norm"][None], final_norm=last, hgrn=hgrn)
    return x.reshape(batch, seq, D_MODEL)


def kernel(x_prompt, x_sample, norm_w, ffn_gate, ffn_up, ffn_down, sgu_w_in, sgu_ln_g, sgu_ln_b, sgu_w_s,
           sgu_b_s, sgu_w_out, hgrn_w_in, hgrn_lb_raw, hgrn_norm_w, hgrn_w_out, final_norm):
    w = {
        "norm_w": norm_w,
        "ffn_gate": ffn_gate.astype(_BF16),
        "ffn_up": ffn_up.astype(_BF16),
        "ffn_down": ffn_down.astype(_BF16),
        "sgu_w_in": sgu_w_in.astype(_BF16),
        "sgu_ln_g": sgu_ln_g,
        "sgu_ln_b": sgu_ln_b,
        "sgu_w_s": sgu_w_s.astype(_BF16),
        "sgu_b_s": jnp.repeat(jnp.swapaxes(sgu_b_s, 1, 2), SGU_GROUP_DIM, axis=2),
        "sgu_w_out": sgu_w_out.astype(_BF16),
        "hgrn_w_in": hgrn_w_in.astype(_BF16),
        "hgrn_lb_raw": hgrn_lb_raw,
        "hgrn_norm_w": hgrn_norm_w,
        "hgrn_w_out": hgrn_w_out.astype(_BF16),
        "final_norm": final_norm,
    }
    return _trunk(x_prompt, w), _trunk(x_sample, w)
```

```python
import functools

import jax
import jax.numpy as jnp
from jax import lax
from jax.experimental import pallas as pl
from jax.experimental.pallas import tpu as pltpu

D_MODEL = 1024
FFN_DIM = 2816
SGU_DIM = 3 * D_MODEL
SGU_GROUPS = 8
SGU_GROUP_DIM = SGU_DIM // SGU_GROUPS
SGU_CHUNK = 128
HEAD_DIM = 128
HEADS = D_MODEL // HEAD_DIM
EPS = 1e-6

SCAN_CHUNK = 64
SUB_BLOCK = 16
SUB_REF = SUB_BLOCK // 2
SCORE_GROUP = 2
SCAN_SEQS = 8
SCAN_LAG = 5

FFN_ROWS = 512
FFN_PARTS = 2
FFN_HGRN_ROWS = 512
SGU_ROWS = 512
SGU_PART = 256
HGRN_ROWS = 512
HGRN_PART = 256

V7X_VMEM_LIMIT_BYTES = 56 * 1024 * 1024

_F32 = jnp.float32
_BF16 = jnp.bfloat16
_NT = (((1,), (1,)), ((), ()))
_TN = (((0,), (0,)), ((), ()))


def _params(*semantics):
    return pltpu.CompilerParams(dimension_semantics=semantics, vmem_limit_bytes=V7X_VMEM_LIMIT_BYTES)


def _row_spec(rows, cols):
    return pl.BlockSpec((rows, cols), lambda i: (i, 0))


def _resident_spec(shape):
    zeros = (0,) * len(shape)
    return pl.BlockSpec(shape, lambda *_: zeros, pipeline_mode=pl.Buffered(1))


def _rmsnorm(x, w):
    return x * lax.rsqrt(jnp.mean(x * x, axis=-1, keepdims=True) + EPS) * w


def _sigmoid(x):
    return 1.0 / (1.0 + jnp.exp(-x))


def _dot(a, b):
    return jnp.dot(a, b, preferred_element_type=_F32)


def _hgrn_mix(o, gate, nw, wout):
    parts = []
    for h in range(HEADS):
        oh = o[:, h * HEAD_DIM:(h + 1) * HEAD_DIM]
        parts.append(oh * lax.rsqrt(jnp.mean(oh * oh, axis=-1, keepdims=True) + EPS))
    o = jnp.concatenate(parts, axis=1) * nw * gate.astype(_F32)
    return _dot(o.astype(_BF16), wout)


def _ffn_kernel(*refs, final_norm, after_hgrn):
    if after_hgrn:
        of_ref, ob_ref, gs_ref, hnw_ref, hwo_ref, x_ref, nw_ref, wg_ref, wu_ref, wd_ref, fw_ref, o_ref = refs
    else:
        x_ref, nw_ref, wg_ref, wu_ref, wd_ref, fw_ref, o_ref = refs

    def residual(rows):
        x = x_ref[rows, :]
        if after_hgrn:
            x = x + _hgrn_mix(of_ref[rows, :] + ob_ref[rows, :], gs_ref[rows, :], hnw_ref[...], hwo_ref[...])
        return x

    def expand(x):
        h = _rmsnorm(x, nw_ref[...]).astype(_BF16)
        gate = _dot(h, wg_ref[...])
        up = _dot(h, wu_ref[...])
        return x, (gate * _sigmoid(gate) * up).astype(_BF16)

    def contract(rows, x, act):
        y = x + 0.5 * _dot(act, wd_ref[...])
        if final_norm:
            y = _rmsnorm(y, fw_ref[...])
        o_ref[rows, :] = y

    part = x_ref.shape[0] // FFN_PARTS
    parts = [slice(n * part, (n + 1) * part) for n in range(FFN_PARTS)]
    inputs = [residual(rows) for rows in parts]
    pending = expand(inputs[0])
    for previous, x in zip(parts, inputs[1:]):
        following = expand(x)
        contract(previous, *pending)
        pending = following
    contract(parts[-1], *pending)


def _ffn(x, nw, wg, wu, wd, fw, *, final_norm, hgrn=None):
    rows = x.shape[0]
    step = FFN_ROWS if hgrn is None else FFN_HGRN_ROWS
    spec = _row_spec(step, D_MODEL)
    hgrn_specs = [] if hgrn is None else [spec, spec, spec, _resident_spec((1, D_MODEL)),
                                          _resident_spec((D_MODEL, D_MODEL))]
    return pl.pallas_call(
        functools.partial(_ffn_kernel, final_norm=final_norm, after_hgrn=hgrn is not None),
        out_shape=jax.ShapeDtypeStruct(x.shape, _F32),
        grid=(rows // step,),
        in_specs=hgrn_specs + [
            spec,
            _resident_spec((1, D_MODEL)),
            _resident_spec((D_MODEL, FFN_DIM)),
            _resident_spec((D_MODEL, FFN_DIM)),
            _resident_spec((FFN_DIM, D_MODEL)),
            _resident_spec((1, D_MODEL)),
        ],
        out_specs=spec,
        compiler_params=_params("parallel"),
        name="ffn" if hgrn is None else "hgrn_out_ffn",
    )(*(hgrn or ()), x, nw, wg, wu, wd, fw)


def _sgu_kernel(x_ref, nw_ref, win_ref, lng_ref, lnb_ref, ws_ref, bs_ref, wout_ref, o_ref):
    def project(n):
        rows = slice(n * SGU_PART, (n + 1) * SGU_PART)
        h = _rmsnorm(x_ref[rows, :], nw_ref[...]).astype(_BF16)
        def gelu(z):
            return 0.5 * z * (1.0 + lax.erf(z * (2.0 ** -0.5)))

        v = gelu(_dot(h, win_ref[:, SGU_DIM:]))
        u = gelu(_dot(h, win_ref[:, :SGU_DIM]))
        mu = jnp.mean(v, axis=-1, keepdims=True)
        vc = v - mu
        v = vc * lax.rsqrt(jnp.mean(vc * vc, axis=-1, keepdims=True) + EPS) * lng_ref[...] + lnb_ref[...]
        return u.astype(_BF16), v.astype(_BF16)

    def mix(n, u, v):
        gated = []
        for c in range(SGU_PART // SGU_CHUNK):
            chunk = slice(c * SGU_CHUNK, (c + 1) * SGU_CHUNK)
            parts = []
            for g in range(SGU_GROUPS):
                cols = slice(g * SGU_GROUP_DIM, (g + 1) * SGU_GROUP_DIM)
                parts.append(_dot(ws_ref[g], v[chunk, cols]))
            s = jnp.concatenate(parts, axis=1) + bs_ref[...]
            gated.append((u[chunk].astype(_F32) * s).astype(_BF16))
        rows = slice(n * SGU_PART, (n + 1) * SGU_PART)
        o_ref[rows, :] = x_ref[rows, :] + _dot(jnp.concatenate(gated, axis=0), wout_ref[...])

    nparts = SGU_ROWS // SGU_PART
    pending = project(0)
    for n in range(1, nparts):
        following = project(n)
        mix(n - 1, *pending)
        pending = following
    mix(nparts - 1, *pending)


def _sgu(x, nw, win, lng, lnb, ws, bs, wout):
    rows = x.shape[0]
    return pl.pallas_call(
        _sgu_kernel,
        out_shape=jax.ShapeDtypeStruct(x.shape, _F32),
        grid=(rows // SGU_ROWS,),
        in_specs=[
            _row_spec(SGU_ROWS, D_MODEL),
            _resident_spec((1, D_MODEL)),
            _resident_spec((D_MODEL, 2 * SGU_DIM)),
            _resident_spec((1, SGU_DIM)),
            _resident_spec((1, SGU_DIM)),
            _resident_spec((SGU_GROUPS, SGU_CHUNK, SGU_CHUNK)),
            _resident_spec((SGU_CHUNK, SGU_DIM)),
            _resident_spec((SGU_DIM, D_MODEL)),
        ],
        out_specs=_row_spec(SGU_ROWS, D_MODEL),
        compiler_params=_params("parallel"),
        name="sgu",
    )(x, nw, win, lng, lnb, ws, bs, wout)


def _lower_bound(raw, layer):
    e = jnp.exp(raw - jnp.max(raw, axis=0, keepdims=True))
    p = e / jnp.sum(e, axis=0, keepdims=True)
    lb = jnp.zeros_like(p[0:1])
    for j in range(1, layer + 1):
        lb = lb + p[j:j + 1]
    return lb


CUMSUM_TERMS = 2


def _chunk_cumsum(ones_bf16, x):
    out = []
    for c in range(x.shape[0] // SCAN_CHUNK):
        rest = x[c * SCAN_CHUNK:(c + 1) * SCAN_CHUNK]
        terms = []
        for _ in range(CUMSUM_TERMS):
            terms.append(rest.astype(_BF16))
            rest = rest - terms[-1].astype(_F32)
        out.append(_dot(ones_bf16, jnp.concatenate(terms, axis=0)))
    return jnp.concatenate(out, axis=0)


def _hgrn_pre_kernel(x_ref, nw_ref, win_ref, lbf_ref, lbb_ref,
                     q_ref, kf_ref, kb_ref, v_ref, gs_ref, bf_ref, bb_ref, *, layer):
    row = lax.broadcasted_iota(jnp.int32, (SCAN_CHUNK, CUMSUM_TERMS * SCAN_CHUNK), 0)
    col = lax.broadcasted_iota(jnp.int32, (SCAN_CHUNK, CUMSUM_TERMS * SCAN_CHUNK), 1) % SCAN_CHUNK
    prefix = jnp.where(col <= row, 1.0, 0.0).astype(_BF16)
    suffix = jnp.where(col >= row, 1.0, 0.0).astype(_BF16)
    lb_f = _lower_bound(lbf_ref[...], layer)
    lb_b = _lower_bound(lbb_ref[...], layer)

    def project(rows):
        h = _rmsnorm(x_ref[rows, :], nw_ref[...]).astype(_BF16)

        def columns(part):
            return _dot(h, win_ref[:, part * D_MODEL:(part + 1) * D_MODEL])

        logit_f, logit_b = columns(1), columns(2)
        q = columns(0)
        q_ref[rows, :] = (q * _sigmoid(q)).astype(_BF16)
        v_ref[rows, :] = columns(3).astype(_BF16)
        g = columns(4)
        gs_ref[rows, :] = (g * _sigmoid(g)).astype(_BF16)
        f_f = lb_f + (1.0 - lb_f) * _sigmoid(logit_f)
        kf_ref[rows, :] = (1.0 - f_f).astype(_BF16)
        f_b = lb_b + (1.0 - lb_b) * _sigmoid(logit_b)
        kb_ref[rows, :] = (1.0 - f_b).astype(_BF16)
        return jnp.log2(f_f), jnp.log2(f_b)

    def accumulate(rows, log_f, log_b):
        bf_ref[rows, :] = _chunk_cumsum(prefix, log_f)
        bb_ref[rows, :] = _chunk_cumsum(suffix, log_b)

    for n in range(HGRN_ROWS // HGRN_PART):
        rows = slice(n * HGRN_PART, (n + 1) * HGRN_PART)
        accumulate(rows, *project(rows))


def _hgrn_pre(x, nw, win, lbf_raw, lbb_raw, *, layer):
    rows = x.shape[0]
    depth = lbf_raw.shape[0]
    half = jax.ShapeDtypeStruct(x.shape, _BF16)
    full = jax.ShapeDtypeStruct(x.shape, _F32)
    spec = _row_spec(HGRN_ROWS, D_MODEL)
    return pl.pallas_call(
        functools.partial(_hgrn_pre_kernel, layer=layer),
        out_shape=(half, half, half, half, half, full, full),
        grid=(rows // HGRN_ROWS,),
        in_specs=[
            spec,
            _resident_spec((1, D_MODEL)),
            _resident_spec((D_MODEL, 5 * D_MODEL)),
            _resident_spec((depth, D_MODEL)),
            _resident_spec((depth, D_MODEL)),
        ],
        out_specs=(spec,) * 7,
        compiler_params=_params("parallel"),
        name="hgrn_pre",
    )(x, nw, win, lbf_raw, lbb_raw)


def _scan_scores(q_ref, k_ref, v_ref, b_ref, st_ref, seq, head, slot, reverse):
    nsub = SCAN_CHUNK // SUB_BLOCK
    exit_row = 0 if reverse else SCAN_CHUNK - 1
    hs = slice(head * HEAD_DIM, (head + 1) * HEAD_DIM)
    b = b_ref[seq, :, hs]
    q = q_ref[seq, :, hs].astype(_F32)
    k = k_ref[seq, :, hs].astype(_F32)
    v = v_ref[seq, :, hs]
    b_exit = b[exit_row:exit_row + 1]
    state = st_ref[slot]
    q_in = (q * jnp.exp2(b)).astype(_BF16)
    o_inter = _dot(q_in, state.astype(_BF16))
    zero_block = jnp.zeros((SUB_BLOCK, HEAD_DIM), _BF16)
    score_rows = []
    for first in range(0, nsub, SCORE_GROUP):
        q_blocks, k_blocks = [], []
        for slot_in_group in range(SCORE_GROUP):
            blk = first + slot_in_group
            rs = slice(blk * SUB_BLOCK, (blk + 1) * SUB_BLOCK)
            cs = slice(blk * SUB_BLOCK, SCAN_CHUNK) if reverse else slice(0, (blk + 1) * SUB_BLOCK)
            b_mid = b[blk * SUB_BLOCK + SUB_REF:blk * SUB_BLOCK + SUB_REF + 1]
            q_row = [zero_block] * SCORE_GROUP
            q_row[slot_in_group] = (q[rs] * jnp.exp2(b[rs] - b_mid)).astype(_BF16)
            q_blocks.append(jnp.concatenate(q_row, axis=1))
            k_col = [zero_block] * nsub
            k_col[cs.start // SUB_BLOCK:cs.stop // SUB_BLOCK] = [(k[cs] * jnp.exp2(b_mid - b[cs])).astype(_BF16)]
            k_blocks.append(jnp.concatenate(k_col, axis=0))
        score_rows.append(lax.dot_general(jnp.concatenate(q_blocks, axis=0), jnp.concatenate(k_blocks, axis=1),
                                          _NT, preferred_element_type=_F32))
    scores = jnp.concatenate(score_rows, axis=0)
    k_out = (k * jnp.exp2(b_exit - b)).astype(_BF16)
    update = lax.dot_general(k_out, v, _TN, preferred_element_type=_F32)
    decay = jnp.transpose(jnp.broadcast_to(jnp.exp2(b_exit), (HEAD_DIM, HEAD_DIM)))
    st_ref[slot] = state * decay + update
    return scores, o_inter, v


def _scan_kernel(qf_ref, kf_ref, vf_ref, bf_ref, qb_ref, kb_ref, vb_ref, bb_ref, of_ref, ob_ref, st_ref):
    @pl.when(pl.program_id(1) == 0)
    def _():
        st_ref[...] = jnp.zeros_like(st_ref)

    t_idx = lax.broadcasted_iota(jnp.int32, (SCAN_CHUNK, SCAN_CHUNK), 0)
    s_idx = lax.broadcasted_iota(jnp.int32, (SCAN_CHUNK, SCAN_CHUNK), 1)
    streams = []
    for seq in range(qf_ref.shape[0]):
        streams.append((seq, False, (qf_ref, kf_ref, vf_ref, bf_ref), of_ref, s_idx <= t_idx))
        streams.append((seq, True, (qb_ref, kb_ref, vb_ref, bb_ref), ob_ref, s_idx >= t_idx))

    items = [(index, h) for index in range(len(streams)) for h in range(HEADS)]

    def stage1(index, h):
        seq, reverse, refs, _, _ = streams[index]
        return _scan_scores(*refs, st_ref, seq, h, index * HEADS + h, reverse)

    def stage2(index, h, scores, o_inter, v):
        seq, _, _, o_ref, seen = streams[index]
        o_ref[seq, :, h * HEAD_DIM:(h + 1) * HEAD_DIM] = (
            o_inter + _dot(jnp.where(seen, scores, 0.0).astype(_BF16), v))

    staged = {}
    for step in range(len(items) + SCAN_LAG):
        if step < len(items):
            staged[step] = stage1(*items[step])
        if step >= SCAN_LAG:
            stage2(*items[step - SCAN_LAG], *staged.pop(step - SCAN_LAG))


def _scan(q, kf, kb, v, bf, bb, *, batch):
    seq = q.shape[0] // batch
    seqs = min(SCAN_SEQS, batch)
    assert batch % seqs == 0 and seq % SCAN_CHUNK == 0, (batch, seq)
    nblocks = seq // SCAN_CHUNK
    shape3 = (batch, seq, D_MODEL)
    block = (seqs, SCAN_CHUNK, D_MODEL)
    fwd = pl.BlockSpec(block, lambda i, n: (i, n, 0))
    bwd = pl.BlockSpec(block, lambda i, n: (i, nblocks - 1 - n, 0))
    q, kf, kb, v, bf, bb = (a.reshape(shape3) for a in (q, kf, kb, v, bf, bb))
    out = jax.ShapeDtypeStruct(shape3, _F32)
    o_f, o_b = pl.pallas_call(
        _scan_kernel,
        out_shape=(out, out),
        grid=(batch // seqs, nblocks),
        in_specs=[fwd] * 4 + [bwd] * 4,
        out_specs=(fwd, bwd),
        scratch_shapes=[pltpu.VMEM((2 * seqs * HEADS, HEAD_DIM, HEAD_DIM), _F32)],
        compiler_params=_params("parallel", "arbitrary"),
        name="scan",
    )(q, kf, v, bf, q, kb, v, bb)
    return o_f.reshape(batch * seq, D_MODEL), o_b.reshape(batch * seq, D_MODEL)


def _trunk(x, w):
    batch, seq, _ = x.shape
    assert (batch * seq) % max(FFN_ROWS, FFN_HGRN_ROWS, SGU_ROWS, HGRN_ROWS) == 0 and seq % SGU_CHUNK == 0, x.shape
    x = x.reshape(batch * seq, D_MODEL)
    depth = w["norm_w"].shape[0]
    for layer in range(depth):
        last = layer == depth - 1
        j = layer // 2
        x = _ffn(x, w["norm_w"][layer, 0][None], w["ffn_gate"][layer, 0], w["ffn_up"][layer, 0],
                 w["ffn_down"][layer, 0], w["final_norm"][None], final_norm=False)
        hgrn = None
        if layer % 2 == 0:
            x = _sgu(x, w["norm_w"][layer, 1][None], w["sgu_w_in"][j], w["sgu_ln_g"][j][None],
                     w["sgu_ln_b"][j][None], w["sgu_w_s"][j], w["sgu_b_s"][j], w["sgu_w_out"][j])
        else:
            q, kf, kb, v, gs, bf, bb = _hgrn_pre(x, w["norm_w"][layer, 1][None], w["hgrn_w_in"][j],
                                                 w["hgrn_lb_raw"][0], w["hgrn_lb_raw"][1], layer=layer)
            o_f, o_b = _scan(q, kf, kb, v, bf, bb, batch=batch)
            hgrn = (o_f, o_b, gs, w["hgrn_norm_w"][j][None], w["hgrn_w_out"][j])
        x = _ffn(x, w["norm_w"][layer, 2][None], w["ffn_gate"][layer, 1], w["ffn_up"][layer, 1],
                 w["ffn_down"][layer, 1], w["final_norm"][None], final_norm=last, hgrn=hgrn)
    return x.reshape(batch, seq, D_MODEL)


def kernel(x_prompt, x_sample, norm_w, ffn_gate, ffn_up, ffn_down, sgu_w_in, sgu_ln_g, sgu_ln_b, sgu_w_s,
           sgu_b_s, sgu_w_out, hgrn_w_in, hgrn_lb_raw, hgrn_norm_w, hgrn_w_out, final_norm):
    w = {
        "norm_w": norm_w,
        "ffn_gate": ffn_gate.astype(_BF16),
        "ffn_up": ffn_up.astype(_BF16),
        "ffn_down": ffn_down.astype(_BF16),
        "sgu_w_in": sgu_w_in.astype(_BF16),
        "sgu_ln_g": sgu_ln_g,
        "sgu_ln_b": sgu_ln_b,
        "sgu_w_s": sgu_w_s.astype(_BF16),
        "sgu_b_s": jnp.repeat(jnp.swapaxes(sgu_b_s, 1, 2), SGU_GROUP_DIM, axis=2),
        "sgu_w_out": sgu_w_out.astype(_BF16),
        "hgrn_w_in": hgrn_w_in.astype(_BF16),
        "hgrn_lb_raw": hgrn_lb_raw,
        "hgrn_norm_w": hgrn_norm_w,
        "hgrn_w_out": hgrn_w_out.astype(_BF16),
        "final_norm": final_norm,
    }
    return _trunk(x_prompt, w), _trunk(x_sample, w)
```

```python
import functools

import jax
import jax.numpy as jnp
from jax import lax
from jax.experimental import pallas as pl
from jax.experimental.pallas import tpu as pltpu

D_MODEL = 1024
FFN_DIM = 2816
SGU_DIM = 3 * D_MODEL
SGU_GROUPS = 8
SGU_GROUP_DIM = SGU_DIM // SGU_GROUPS
SGU_CHUNK = 128
HEAD_DIM = 128
HEADS = D_MODEL // HEAD_DIM
EPS = 1e-6

SCAN_CHUNK = 64
SUB_BLOCK = 16
SUB_REF = SUB_BLOCK // 2
SCORE_GROUP = 2
SCAN_SEQS = 8
SCAN_LAG = 5

FFN_ROWS = 512
FFN_PARTS = 2
FFN_HGRN_ROWS = 512
SGU_ROWS = 512
SGU_PART = 256
HGRN_ROWS = 512
HGRN_PART = 256

V7X_VMEM_LIMIT_BYTES = 56 * 1024 * 1024

_F32 = jnp.float32
_BF16 = jnp.bfloat16
_NT = (((1,), (1,)), ((), ()))
_TN = (((0,), (0,)), ((), ()))


def _params(*semantics):
    return pltpu.CompilerParams(dimension_semantics=semantics, vmem_limit_bytes=V7X_VMEM_LIMIT_BYTES)


def _row_spec(rows, cols):
    return pl.BlockSpec((rows, cols), lambda i: (i, 0))


def _resident_spec(shape):
    zeros = (0,) * len(shape)
    return pl.BlockSpec(shape, lambda *_: zeros, pipeline_mode=pl.Buffered(1))


def _rmsnorm(x, w):
    return x * lax.rsqrt(jnp.mean(x * x, axis=-1, keepdims=True) + EPS) * w


def _sigmoid(x):
    return 1.0 / (1.0 + jnp.exp(-x))


def _dot(a, b):
    return jnp.dot(a, b, preferred_element_type=_F32)


def _hgrn_mix(o, gate, nw, wout):
    parts = []
    for h in range(HEADS):
        oh = o[:, h * HEAD_DIM:(h + 1) * HEAD_DIM]
        parts.append(oh * lax.rsqrt(jnp.mean(oh * oh, axis=-1, keepdims=True) + EPS))
    o = jnp.concatenate(parts, axis=1) * nw * gate.astype(_F32)
    return _dot(o.astype(_BF16), wout)


def _ffn_kernel(*refs, final_norm, after_hgrn):
    if after_hgrn:
        of_ref, ob_ref, gs_ref, hnw_ref, hwo_ref, x_ref, nw_ref, wg_ref, wu_ref, wd_ref, fw_ref, o_ref = refs
    else:
        x_ref, nw_ref, wg_ref, wu_ref, wd_ref, fw_ref, o_ref = refs

    def residual(rows):
        x = x_ref[rows, :]
        if after_hgrn:
            x = x + _hgrn_mix(of_ref[rows, :] + ob_ref[rows, :], gs_ref[rows, :], hnw_ref[...], hwo_ref[...])
        return x

    def expand(x):
        h = _rmsnorm(x, nw_ref[...]).astype(_BF16)
        gate = _dot(h, wg_ref[...])
        up = _dot(h, wu_ref[...])
        return x, (gate * _sigmoid(gate) * up).astype(_BF16)

    def contract(rows, x, act):
        y = x + 0.5 * _dot(act, wd_ref[...])
        if final_norm:
            y = _rmsnorm(y, fw_ref[...])
        o_ref[rows, :] = y

    part = x_ref.shape[0] // FFN_PARTS
    parts = [slice(n * part, (n + 1) * part) for n in range(FFN_PARTS)]
    inputs = [residual(rows) for rows in parts]
    pending = expand(inputs[0])
    for previous, x in zip(parts, inputs[1:]):
        following = expand(x)
        contract(previous, *pending)
        pending = following
    contract(parts[-1], *pending)


def _ffn(x, nw, wg, wu, wd, fw, *, final_norm, hgrn=None):
    rows = x.shape[0]
    step = FFN_ROWS if hgrn is None else FFN_HGRN_ROWS
    spec = _row_spec(step, D_MODEL)
    hgrn_specs = [] if hgrn is None else [spec, spec, spec, _resident_spec((1, D_MODEL)),
                                          _resident_spec((D_MODEL, D_MODEL))]
    return pl.pallas_call(
        functools.partial(_ffn_kernel, final_norm=final_norm, after_hgrn=hgrn is not None),
        out_shape=jax.ShapeDtypeStruct(x.shape, _F32),
        grid=(rows // step,),
        in_specs=hgrn_specs + [
            spec,
            _resident_spec((1, D_MODEL)),
            _resident_spec((D_MODEL, FFN_DIM)),
            _resident_spec((D_MODEL, FFN_DIM)),
            _resident_spec((FFN_DIM, D_MODEL)),
            _resident_spec((1, D_MODEL)),
        ],
        out_specs=spec,
        compiler_params=_params("parallel"),
        name="ffn" if hgrn is None else "hgrn_out_ffn",
    )(*(hgrn or ()), x, nw, wg, wu, wd, fw)


def _sgu_kernel(x_ref, nw_ref, win_ref, lng_ref, lnb_ref, ws_ref, bs_ref, wout_ref, o_ref):
    def project(n):
        rows = slice(n * SGU_PART, (n + 1) * SGU_PART)
        h = _rmsnorm(x_ref[rows, :], nw_ref[...]).astype(_BF16)
        def gelu(z):
            return 0.5 * z * (1.0 + lax.erf(z * (2.0 ** -0.5)))

        v = gelu(_dot(h, win_ref[:, SGU_DIM:]))
        u = gelu(_dot(h, win_ref[:, :SGU_DIM]))
        mu = jnp.mean(v, axis=-1, keepdims=True)
        vc = v - mu
        v = vc * lax.rsqrt(jnp.mean(vc * vc, axis=-1, keepdims=True) + EPS) * lng_ref[...] + lnb_ref[...]
        return u.astype(_BF16), v.astype(_BF16)

    def mix(n, u, v):
        gated = []
        for c in range(SGU_PART // SGU_CHUNK):
            chunk = slice(c * SGU_CHUNK, (c + 1) * SGU_CHUNK)
            parts = []
            for g in range(SGU_GROUPS):
                cols = slice(g * SGU_GROUP_DIM, (g + 1) * SGU_GROUP_DIM)
                parts.append(_dot(ws_ref[g], v[chunk, cols]))
            s = jnp.concatenate(parts, axis=1) + bs_ref[...]
            gated.append((u[chunk].astype(_F32) * s).astype(_BF16))
        rows = slice(n * SGU_PART, (n + 1) * SGU_PART)
        o_ref[rows, :] = x_ref[rows, :] + _dot(jnp.concatenate(gated, axis=0), wout_ref[...])

    nparts = SGU_ROWS // SGU_PART
    pending = project(0)
    for n in range(1, nparts):
        following = project(n)
        mix(n - 1, *pending)
        pending = following
    mix(nparts - 1, *pending)


def _sgu(x, nw, win, lng, lnb, ws, bs, wout):
    rows = x.shape[0]
    return pl.pallas_call(
        _sgu_kernel,
        out_shape=jax.ShapeDtypeStruct(x.shape, _F32),
        grid=(rows // SGU_ROWS,),
        in_specs=[
            _row_spec(SGU_ROWS, D_MODEL),
            _resident_spec((1, D_MODEL)),
            _resident_spec((D_MODEL, 2 * SGU_DIM)),
            _resident_spec((1, SGU_DIM)),
            _resident_spec((1, SGU_DIM)),
            _resident_spec((SGU_GROUPS, SGU_CHUNK, SGU_CHUNK)),
            _resident_spec((SGU_CHUNK, SGU_DIM)),
            _resident_spec((SGU_DIM, D_MODEL)),
        ],
        out_specs=_row_spec(SGU_ROWS, D_MODEL),
        compiler_params=_params("parallel"),
        name="sgu",
    )(x, nw, win, lng, lnb, ws, bs, wout)


def _lower_bound(raw, layer):
    e = jnp.exp(raw - jnp.max(raw, axis=0, keepdims=True))
    p = e / jnp.sum(e, axis=0, keepdims=True)
    lb = jnp.zeros_like(p[0:1])
    for j in range(1, layer + 1):
        lb = lb + p[j:j + 1]
    return lb


CUMSUM_TERMS = 2


def _chunk_cumsum(ones_bf16, x):
    out = []
    for c in range(x.shape[0] // SCAN_CHUNK):
        rest = x[c * SCAN_CHUNK:(c + 1) * SCAN_CHUNK]
        terms = []
        for _ in range(CUMSUM_TERMS):
            terms.append(rest.astype(_BF16))
            rest = rest - terms[-1].astype(_F32)
        out.append(_dot(ones_bf16, jnp.concatenate(terms, axis=0)))
    return jnp.concatenate(out, axis=0)


def _hgrn_pre_kernel(x_ref, nw_ref, win_ref, lbf_ref, lbb_ref,
                     q_ref, kf_ref, kb_ref, v_ref, gs_ref, bf_ref, bb_ref, *, layer):
    row = lax.broadcasted_iota(jnp.int32, (SCAN_CHUNK, CUMSUM_TERMS * SCAN_CHUNK), 0)
    col = lax.broadcasted_iota(jnp.int32, (SCAN_CHUNK, CUMSUM_TERMS * SCAN_CHUNK), 1) % SCAN_CHUNK
    prefix = jnp.where(col <= row, 1.0, 0.0).astype(_BF16)
    suffix = jnp.where(col >= row, 1.0, 0.0).astype(_BF16)
    lb_f = _lower_bound(lbf_ref[...], layer)
    lb_b = _lower_bound(lbb_ref[...], layer)

    def project(rows):
        h = _rmsnorm(x_ref[rows, :], nw_ref[...]).astype(_BF16)

        def columns(part):
            return _dot(h, win_ref[:, part * D_MODEL:(part + 1) * D_MODEL])

        logit_f, logit_b = columns(1), columns(2)
        q = columns(0)
        q_ref[rows, :] = (q * _sigmoid(q)).astype(_BF16)
        v_ref[rows, :] = columns(3).astype(_BF16)
        g = columns(4)
        gs_ref[rows, :] = (g * _sigmoid(g)).astype(_BF16)
        f_f = lb_f + (1.0 - lb_f) * _sigmoid(logit_f)
        kf_ref[rows, :] = (1.0 - f_f).astype(_BF16)
        f_b = lb_b + (1.0 - lb_b) * _sigmoid(logit_b)
        kb_ref[rows, :] = (1.0 - f_b).astype(_BF16)
        return jnp.log2(f_f), jnp.log2(f_b)

    def accumulate(rows, log_f, log_b):
        bf_ref[rows, :] = _chunk_cumsum(prefix, log_f)
        bb_ref[rows, :] = _chunk_cumsum(suffix, log_b)

    for n in range(HGRN_ROWS // HGRN_PART):
        rows = slice(n * HGRN_PART, (n + 1) * HGRN_PART)
        accumulate(rows, *project(rows))


def _hgrn_pre(x, nw, win, lbf_raw, lbb_raw, *, layer):
    rows = x.shape[0]
    depth = lbf_raw.shape[0]
    half = jax.ShapeDtypeStruct(x.shape, _BF16)
    full = jax.ShapeDtypeStruct(x.shape, _F32)
    spec = _row_spec(HGRN_ROWS, D_MODEL)
    return pl.pallas_call(
        functools.partial(_hgrn_pre_kernel, layer=layer),
        out_shape=(half, half, half, half, half, full, full),
        grid=(rows // HGRN_ROWS,),
        in_specs=[
            spec,
            _resident_spec((1, D_MODEL)),
            _resident_spec((D_MODEL, 5 * D_MODEL)),
            _resident_spec((depth, D_MODEL)),
            _resident_spec((depth, D_MODEL)),
        ],
        out_specs=(spec,) * 7,
        compiler_params=_params("parallel"),
        name="hgrn_pre",
    )(x, nw, win, lbf_raw, lbb_raw)


def _scan_scores(q_ref, k_ref, v_ref, b_ref, st_ref, seq, rows, head, slot, reverse):
    nsub = SCAN_CHUNK // SUB_BLOCK
    exit_row = 0 if reverse else SCAN_CHUNK - 1
    hs = slice(head * HEAD_DIM, (head + 1) * HEAD_DIM)
    b = b_ref[seq, rows, hs]
    q = q_ref[seq, rows, hs].astype(_F32)
    k = k_ref[seq, rows, hs].astype(_F32)
    v = v_ref[seq, rows, hs]
    b_exit = b[exit_row:exit_row + 1]
    state = st_ref[slot]
    q_in = (q * jnp.exp2(b)).astype(_BF16)
    o_inter = _dot(q_in, state.astype(_BF16))
    zero_block = jnp.zeros((SUB_BLOCK, HEAD_DIM), _BF16)
    score_rows = []
    for first in range(0, nsub, SCORE_GROUP):
        q_blocks, k_blocks = [], []
        for slot_in_group in range(SCORE_GROUP):
            blk = first + slot_in_group
            rs = slice(blk * SUB_BLOCK, (blk + 1) * SUB_BLOCK)
            cs = slice(blk * SUB_BLOCK, SCAN_CHUNK) if reverse else slice(0, (blk + 1) * SUB_BLOCK)
            b_mid = b[blk * SUB_BLOCK + SUB_REF:blk * SUB_BLOCK + SUB_REF + 1]
            q_row = [zero_block] * SCORE_GROUP
            q_row[slot_in_group] = (q[rs] * jnp.exp2(b[rs] - b_mid)).astype(_BF16)
            q_blocks.append(jnp.concatenate(q_row, axis=1))
            k_col = [zero_block] * nsub
            k_col[cs.start // SUB_BLOCK:cs.stop // SUB_BLOCK] = [(k[cs] * jnp.exp2(b_mid - b[cs])).astype(_BF16)]
            k_blocks.append(jnp.concatenate(k_col, axis=0))
        score_rows.append(lax.dot_general(jnp.concatenate(q_blocks, axis=0), jnp.concatenate(k_blocks, axis=1),
                                          _NT, preferred_element_type=_F32))
    scores = jnp.concatenate(score_rows, axis=0)
    k_out = (k * jnp.exp2(b_exit - b)).astype(_BF16)
    update = lax.dot_general(k_out, v, _TN, preferred_element_type=_F32)
    decay = jnp.transpose(jnp.broadcast_to(jnp.exp2(b_exit), (HEAD_DIM, HEAD_DIM)))
    st_ref[slot] = state * decay + update
    return scores, o_inter, v


def _scan_kernel(qf_ref, kf_ref, vf_ref, bf_ref, qb_ref, kb_ref, vb_ref, bb_ref, of_ref, ob_ref, st_ref):
    @pl.when(pl.program_id(1) == 0)
    def _():
        st_ref[...] = jnp.zeros_like(st_ref)

    t_idx = lax.broadcasted_iota(jnp.int32, (SCAN_CHUNK, SCAN_CHUNK), 0)
    s_idx = lax.broadcasted_iota(jnp.int32, (SCAN_CHUNK, SCAN_CHUNK), 1)
    streams = []
    for seq in range(qf_ref.shape[0]):
        streams.append((seq, False, (qf_ref, kf_ref, vf_ref, bf_ref), of_ref, s_idx <= t_idx))
        streams.append((seq, True, (qb_ref, kb_ref, vb_ref, bb_ref), ob_ref, s_idx >= t_idx))

    chunks = qf_ref.shape[1] // SCAN_CHUNK
    items = [(index, h, order) for order in range(chunks) for index in range(len(streams)) for h in range(HEADS)]

    def chunk_rows(reverse, order):
        c = chunks - 1 - order if reverse else order
        return slice(c * SCAN_CHUNK, (c + 1) * SCAN_CHUNK)

    def stage1(index, h, order):
        seq, reverse, refs, _, _ = streams[index]
        return _scan_scores(*refs, st_ref, seq, chunk_rows(reverse, order), h, index * HEADS + h, reverse)

    def stage2(index, h, order, scores, o_inter, v):
        seq, reverse, _, o_ref, seen = streams[index]
        o_ref[seq, chunk_rows(reverse, order), h * HEAD_DIM:(h + 1) * HEAD_DIM] = (
            o_inter + _dot(jnp.where(seen, scores, 0.0).astype(_BF16), v))

    staged = {}
    for step in range(len(items) + SCAN_LAG):
        if step < len(items):
            staged[step] = stage1(*items[step])
        if step >= SCAN_LAG:
            stage2(*items[step - SCAN_LAG], *staged.pop(step - SCAN_LAG))


def _scan(q, kf, kb, v, bf, bb, *, batch):
    seq = q.shape[0] // batch
    seqs = min(SCAN_SEQS, batch)
    chunks = SCAN_SEQS // seqs
    assert batch % seqs == 0 and seq % (chunks * SCAN_CHUNK) == 0, (batch, seq)
    nblocks = seq // (chunks * SCAN_CHUNK)
    shape3 = (batch, seq, D_MODEL)
    block = (seqs, chunks * SCAN_CHUNK, D_MODEL)
    fwd = pl.BlockSpec(block, lambda i, n: (i, n, 0))
    bwd = pl.BlockSpec(block, lambda i, n: (i, nblocks - 1 - n, 0))
    q, kf, kb, v, bf, bb = (a.reshape(shape3) for a in (q, kf, kb, v, bf, bb))
    out = jax.ShapeDtypeStruct(shape3, _F32)
    o_f, o_b = pl.pallas_call(
        _scan_kernel,
        out_shape=(out, out),
        grid=(batch // seqs, nblocks),
        in_specs=[fwd] * 4 + [bwd] * 4,
        out_specs=(fwd, bwd),
        scratch_shapes=[pltpu.VMEM((2 * seqs * HEADS, HEAD_DIM, HEAD_DIM), _F32)],
        compiler_params=_params("parallel", "arbitrary"),
        name="scan",
    )(q, kf, v, bf, q, kb, v, bb)
    return o_f.reshape(batch * seq, D_MODEL), o_b.reshape(batch * seq, D_MODEL)


def _trunk(x, w):
    batch, seq, _ = x.shape
    assert (batch * seq) % max(FFN_ROWS, FFN_HGRN_ROWS, SGU_ROWS, HGRN_ROWS) == 0 and seq % SGU_CHUNK == 0, x.shape
    x = x.reshape(batch * seq, D_MODEL)
    depth = w["norm_w"].shape[0]
    for layer in range(depth):
        last = layer == depth - 1
        j = layer // 2
        x = _ffn(x, w["norm_w"][layer, 0][None], w["ffn_gate"][layer, 0], w["ffn_up"][layer, 0],
                 w["ffn_down"][layer, 0], w["final_norm"][None], final_norm=False)
        hgrn = None
        if layer % 2 == 0:
            x = _sgu(x, w["norm_w"][layer, 1][None], w["sgu_w_in"][j], w["sgu_ln_g"][j][None],
                     w["sgu_ln_b"][j][None], w["sgu_w_s"][j], w["sgu_b_s"][j], w["sgu_w_out"][j])
        else:
            q, kf, kb, v, gs, bf, bb = _hgrn_pre(x, w["norm_w"][layer, 1][None], w["hgrn_w_in"][j],
                                                 w["hgrn_lb_raw"][0], w["hgrn_lb_raw"][1], layer=layer)
            o_f, o_b = _scan(q, kf, kb, v, bf, bb, batch=batch)
            hgrn = (o_f, o_b, gs, w["hgrn_norm_w"][j][None], w["hgrn_w_out"][j])
        x = _ffn(x, w["norm_w"][layer, 2][None], w["ffn_gate"][layer, 1], w["ffn_up"][layer, 1],
                 w["ffn_down"][layer, 1], w["final_norm"][None], final_norm=last, hgrn=hgrn)
    return x.reshape(batch, seq, D_MODEL)


def kernel(x_prompt, x_sample, norm_w, ffn_gate, ffn_up, ffn_down, sgu_w_in, sgu_ln_g, sgu_ln_b, sgu_w_s,
           sgu_b_s, sgu_w_out, hgrn_w_in, hgrn_lb_raw, hgrn_norm_w, hgrn_w_out, final_norm):
    w = {
        "norm_w": norm_w,
        "ffn_gate": ffn_gate.astype(_BF16),
        "ffn_up": ffn_up.astype(_BF16),
        "ffn_down": ffn_down.astype(_BF16),
        "sgu_w_in": sgu_w_in.astype(_BF16),
        "sgu_ln_g": sgu_ln_g,
        "sgu_ln_b": sgu_ln_b,
        "sgu_w_s": sgu_w_s.astype(_BF16),
        "sgu_b_s": jnp.repeat(jnp.swapaxes(sgu_b_s, 1, 2), SGU_GROUP_DIM, axis=2),
        "sgu_w_out": sgu_w_out.astype(_BF16),
        "hgrn_w_in": hgrn_w_in.astype(_BF16),
        "hgrn_lb_raw": hgrn_lb_raw,
        "hgrn_norm_w": hgrn_norm_w,
        "hgrn_w_out": hgrn_w_out.astype(_BF16),
        "final_norm": final_norm,
    }
    return _trunk(x_prompt, w), _trunk(x_sample, w)
```

```python
import functools

import jax
import jax.numpy as jnp
from jax import lax
from jax.experimental import pallas as pl
from jax.experimental.pallas import tpu as pltpu

D_MODEL = 1024
FFN_DIM = 2816
SGU_DIM = 3 * D_MODEL
SGU_GROUPS = 8
SGU_GROUP_DIM = SGU_DIM // SGU_GROUPS
SGU_CHUNK = 128
HEAD_DIM = 128
HEADS = D_MODEL // HEAD_DIM
EPS = 1e-6

SCAN_CHUNK = 64
SUB_BLOCK = 16
SUB_REF = SUB_BLOCK // 2
SCORE_GROUP = 2
SCAN_SEQS = 8
SCAN_LAG = 5

FFN_ROWS = 512
FFN_PARTS = 2
FFN_HGRN_ROWS = 512
SGU_ROWS = 512
SGU_PART = 256
HGRN_ROWS = 512
HGRN_PART = 256

V7X_VMEM_LIMIT_BYTES = 56 * 1024 * 1024

_F32 = jnp.float32
_BF16 = jnp.bfloat16
_NT = (((1,), (1,)), ((), ()))
_TN = (((0,), (0,)), ((), ()))


def _params(*semantics):
    return pltpu.CompilerParams(dimension_semantics=semantics, vmem_limit_bytes=V7X_VMEM_LIMIT_BYTES)


def _row_spec(rows, cols):
    return pl.BlockSpec((rows, cols), lambda i: (i, 0))


def _resident_spec(shape):
    zeros = (0,) * len(shape)
    return pl.BlockSpec(shape, lambda *_: zeros, pipeline_mode=pl.Buffered(1))


def _rmsnorm(x, w):
    return x * lax.rsqrt(jnp.mean(x * x, axis=-1, keepdims=True) + EPS) * w


def _sigmoid(x):
    return 1.0 / (1.0 + jnp.exp(-x))


def _dot(a, b):
    return jnp.dot(a, b, preferred_element_type=_F32)


def _hgrn_mix(o, gate, nw, wout):
    parts = []
    for h in range(HEADS):
        oh = o[:, h * HEAD_DIM:(h + 1) * HEAD_DIM]
        parts.append(oh * lax.rsqrt(jnp.mean(oh * oh, axis=-1, keepdims=True) + EPS))
    o = jnp.concatenate(parts, axis=1) * nw * gate.astype(_F32)
    return _dot(o.astype(_BF16), wout)


def _ffn_kernel(*refs, final_norm, after_hgrn):
    if after_hgrn:
        of_ref, ob_ref, gs_ref, hnw_ref, hwo_ref, x_ref, nw_ref, wg_ref, wu_ref, wd_ref, fw_ref, o_ref = refs
    else:
        x_ref, nw_ref, wg_ref, wu_ref, wd_ref, fw_ref, o_ref = refs

    def residual(rows):
        x = x_ref[rows, :]
        if after_hgrn:
            x = x + _hgrn_mix(of_ref[rows, :] + ob_ref[rows, :], gs_ref[rows, :], hnw_ref[...], hwo_ref[...])
        return x

    def expand(x):
        h = _rmsnorm(x, nw_ref[...]).astype(_BF16)
        gate = _dot(h, wg_ref[...])
        up = _dot(h, wu_ref[...])
        return x, (gate * _sigmoid(gate) * up).astype(_BF16)

    def contract(rows, x, act):
        y = x + 0.5 * _dot(act, wd_ref[...])
        if final_norm:
            y = _rmsnorm(y, fw_ref[...])
        o_ref[rows, :] = y

    part = x_ref.shape[0] // FFN_PARTS
    parts = [slice(n * part, (n + 1) * part) for n in range(FFN_PARTS)]
    inputs = [residual(rows) for rows in parts]
    pending = expand(inputs[0])
    for previous, x in zip(parts, inputs[1:]):
        following = expand(x)
        contract(previous, *pending)
        pending = following
    contract(parts[-1], *pending)


def _ffn(x, nw, wg, wu, wd, fw, *, final_norm, hgrn=None):
    rows = x.shape[0]
    step = FFN_ROWS if hgrn is None else FFN_HGRN_ROWS
    spec = _row_spec(step, D_MODEL)
    hgrn_specs = [] if hgrn is None else [spec, spec, spec, _resident_spec((1, D_MODEL)),
                                          _resident_spec((D_MODEL, D_MODEL))]
    return pl.pallas_call(
        functools.partial(_ffn_kernel, final_norm=final_norm, after_hgrn=hgrn is not None),
        out_shape=jax.ShapeDtypeStruct(x.shape, _F32),
        grid=(rows // step,),
        in_specs=hgrn_specs + [
            spec,
            _resident_spec((1, D_MODEL)),
            _resident_spec((D_MODEL, FFN_DIM)),
            _resident_spec((D_MODEL, FFN_DIM)),
            _resident_spec((FFN_DIM, D_MODEL)),
            _resident_spec((1, D_MODEL)),
        ],
        out_specs=spec,
        compiler_params=_params("parallel"),
        name="ffn" if hgrn is None else "hgrn_out_ffn",
    )(*(hgrn or ()), x, nw, wg, wu, wd, fw)


def _sgu_kernel(x_ref, nw_ref, win_ref, lng_ref, lnb_ref, ws_ref, bs_ref, wout_ref, o_ref):
    def project(n):
        rows = slice(n * SGU_PART, (n + 1) * SGU_PART)
        h = _rmsnorm(x_ref[rows, :], nw_ref[...]).astype(_BF16)
        def gelu(z):
            return 0.5 * z * (1.0 + lax.erf(z * (2.0 ** -0.5)))

        v = gelu(_dot(h, win_ref[:, SGU_DIM:]))
        u = gelu(_dot(h, win_ref[:, :SGU_DIM]))
        mu = jnp.mean(v, axis=-1, keepdims=True)
        vc = v - mu
        v = vc * lax.rsqrt(jnp.mean(vc * vc, axis=-1, keepdims=True) + EPS) * lng_ref[...] + lnb_ref[...]
        return u.astype(_BF16), v.astype(_BF16)

    def mix(n, u, v):
        chunks = [slice(c * SGU_CHUNK, (c + 1) * SGU_CHUNK) for c in range(SGU_PART // SGU_CHUNK)]
        mixed = [[] for _ in chunks]
        for g in range(SGU_GROUPS):
            cols = slice(g * SGU_GROUP_DIM, (g + 1) * SGU_GROUP_DIM)
            both = _dot(ws_ref[g], jnp.concatenate([v[chunk, cols] for chunk in chunks], axis=1))
            for c in range(len(chunks)):
                mixed[c].append(both[:, c * SGU_GROUP_DIM:(c + 1) * SGU_GROUP_DIM])
        gated = []
        for c, chunk in enumerate(chunks):
            s = jnp.concatenate(mixed[c], axis=1) + bs_ref[...]
            gated.append((u[chunk].astype(_F32) * s).astype(_BF16))
        rows = slice(n * SGU_PART, (n + 1) * SGU_PART)
        o_ref[rows, :] = x_ref[rows, :] + _dot(jnp.concatenate(gated, axis=0), wout_ref[...])

    nparts = SGU_ROWS // SGU_PART
    pending = project(0)
    for n in range(1, nparts):
        following = project(n)
        mix(n - 1, *pending)
        pending = following
    mix(nparts - 1, *pending)


def _sgu(x, nw, win, lng, lnb, ws, bs, wout):
    rows = x.shape[0]
    return pl.pallas_call(
        _sgu_kernel,
        out_shape=jax.ShapeDtypeStruct(x.shape, _F32),
        grid=(rows // SGU_ROWS,),
        in_specs=[
            _row_spec(SGU_ROWS, D_MODEL),
            _resident_spec((1, D_MODEL)),
            _resident_spec((D_MODEL, 2 * SGU_DIM)),
            _resident_spec((1, SGU_DIM)),
            _resident_spec((1, SGU_DIM)),
            _resident_spec((SGU_GROUPS, SGU_CHUNK, SGU_CHUNK)),
            _resident_spec((SGU_CHUNK, SGU_DIM)),
            _resident_spec((SGU_DIM, D_MODEL)),
        ],
        out_specs=_row_spec(SGU_ROWS, D_MODEL),
        compiler_params=_params("parallel"),
        name="sgu",
    )(x, nw, win, lng, lnb, ws, bs, wout)


def _lower_bound(raw, layer):
    e = jnp.exp(raw - jnp.max(raw, axis=0, keepdims=True))
    p = e / jnp.sum(e, axis=0, keepdims=True)
    lb = jnp.zeros_like(p[0:1])
    for j in range(1, layer + 1):
        lb = lb + p[j:j + 1]
    return lb


CUMSUM_TERMS = 2


def _chunk_cumsum(ones_bf16, x):
    out = []
    for c in range(x.shape[0] // SCAN_CHUNK):
        rest = x[c * SCAN_CHUNK:(c + 1) * SCAN_CHUNK]
        terms = []
        for _ in range(CUMSUM_TERMS):
            terms.append(rest.astype(_BF16))
            rest = rest - terms[-1].astype(_F32)
        out.append(_dot(ones_bf16, jnp.concatenate(terms, axis=0)))
    return jnp.concatenate(out, axis=0)


def _hgrn_pre_kernel(x_ref, nw_ref, win_ref, lbf_ref, lbb_ref,
                     q_ref, kf_ref, kb_ref, v_ref, gs_ref, bf_ref, bb_ref, *, layer):
    row = lax.broadcasted_iota(jnp.int32, (SCAN_CHUNK, CUMSUM_TERMS * SCAN_CHUNK), 0)
    col = lax.broadcasted_iota(jnp.int32, (SCAN_CHUNK, CUMSUM_TERMS * SCAN_CHUNK), 1) % SCAN_CHUNK
    prefix = jnp.where(col <= row, 1.0, 0.0).astype(_BF16)
    suffix = jnp.where(col >= row, 1.0, 0.0).astype(_BF16)
    lb_f = _lower_bound(lbf_ref[...], layer)
    lb_b = _lower_bound(lbb_ref[...], layer)

    def project(rows):
        h = _rmsnorm(x_ref[rows, :], nw_ref[...]).astype(_BF16)

        def columns(part):
            return _dot(h, win_ref[:, part * D_MODEL:(part + 1) * D_MODEL])

        logit_f, logit_b = columns(1), columns(2)
        q = columns(0)
        q_ref[rows, :] = (q * _sigmoid(q)).astype(_BF16)
        v_ref[rows, :] = columns(3).astype(_BF16)
        g = columns(4)
        gs_ref[rows, :] = (g * _sigmoid(g)).astype(_BF16)
        f_f = lb_f + (1.0 - lb_f) * _sigmoid(logit_f)
        kf_ref[rows, :] = (1.0 - f_f).astype(_BF16)
        f_b = lb_b + (1.0 - lb_b) * _sigmoid(logit_b)
        kb_ref[rows, :] = (1.0 - f_b).astype(_BF16)
        return jnp.log2(f_f), jnp.log2(f_b)

    def accumulate(rows, log_f, log_b):
        bf_ref[rows, :] = _chunk_cumsum(prefix, log_f)
        bb_ref[rows, :] = _chunk_cumsum(suffix, log_b)

    for n in range(HGRN_ROWS // HGRN_PART):
        rows = slice(n * HGRN_PART, (n + 1) * HGRN_PART)
        accumulate(rows, *project(rows))


def _hgrn_pre(x, nw, win, lbf_raw, lbb_raw, *, layer):
    rows = x.shape[0]
    depth = lbf_raw.shape[0]
    half = jax.ShapeDtypeStruct(x.shape, _BF16)
    full = jax.ShapeDtypeStruct(x.shape, _F32)
    spec = _row_spec(HGRN_ROWS, D_MODEL)
    return pl.pallas_call(
        functools.partial(_hgrn_pre_kernel, layer=layer),
        out_shape=(half, half, half, half, half, full, full),
        grid=(rows // HGRN_ROWS,),
        in_specs=[
            spec,
            _resident_spec((1, D_MODEL)),
            _resident_spec((D_MODEL, 5 * D_MODEL)),
            _resident_spec((depth, D_MODEL)),
            _resident_spec((depth, D_MODEL)),
        ],
        out_specs=(spec,) * 7,
        compiler_params=_params("parallel"),
        name="hgrn_pre",
    )(x, nw, win, lbf_raw, lbb_raw)


def _scan_scores(q_ref, k_ref, v_ref, b_ref, st_ref, seq, rows, head, slot, reverse):
    nsub = SCAN_CHUNK // SUB_BLOCK
    exit_row = 0 if reverse else SCAN_CHUNK - 1
    hs = slice(head * HEAD_DIM, (head + 1) * HEAD_DIM)
    b = b_ref[seq, rows, hs]
    q = q_ref[seq, rows, hs].astype(_F32)
    k = k_ref[seq, rows, hs].astype(_F32)
    v = v_ref[seq, rows, hs]
    b_exit = b[exit_row:exit_row + 1]
    state = st_ref[slot]
    q_in = (q * jnp.exp2(b)).astype(_BF16)
    o_inter = _dot(q_in, state.astype(_BF16))
    zero_block = jnp.zeros((SUB_BLOCK, HEAD_DIM), _BF16)
    score_rows = []
    for first in range(0, nsub, SCORE_GROUP):
        q_blocks, k_blocks = [], []
        for slot_in_group in range(SCORE_GROUP):
            blk = first + slot_in_group
            rs = slice(blk * SUB_BLOCK, (blk + 1) * SUB_BLOCK)
            cs = slice(blk * SUB_BLOCK, SCAN_CHUNK) if reverse else slice(0, (blk + 1) * SUB_BLOCK)
            b_mid = b[blk * SUB_BLOCK + SUB_REF:blk * SUB_BLOCK + SUB_REF + 1]
            q_row = [zero_block] * SCORE_GROUP
            q_row[slot_in_group] = (q[rs] * jnp.exp2(b[rs] - b_mid)).astype(_BF16)
            q_blocks.append(jnp.concatenate(q_row, axis=1))
            k_col = [zero_block] * nsub
            k_col[cs.start // SUB_BLOCK:cs.stop // SUB_BLOCK] = [(k[cs] * jnp.exp2(b_mid - b[cs])).astype(_BF16)]
            k_blocks.append(jnp.concatenate(k_col, axis=0))
        score_rows.append(lax.dot_general(jnp.concatenate(q_blocks, axis=0), jnp.concatenate(k_blocks, axis=1),
                                          _NT, preferred_element_type=_F32))
    scores = jnp.concatenate(score_rows, axis=0)
    k_out = (k * jnp.exp2(b_exit - b)).astype(_BF16)
    update = lax.dot_general(k_out, v, _TN, preferred_element_type=_F32)
    decay = jnp.transpose(jnp.broadcast_to(jnp.exp2(b_exit), (HEAD_DIM, HEAD_DIM)))
    st_ref[slot] = state * decay + update
    return scores, o_inter, v


def _scan_kernel(qf_ref, kf_ref, vf_ref, bf_ref, qb_ref, kb_ref, vb_ref, bb_ref, of_ref, ob_ref, st_ref):
    @pl.when(pl.program_id(1) == 0)
    def _():
        st_ref[...] = jnp.zeros_like(st_ref)

    t_idx = lax.broadcasted_iota(jnp.int32, (SCAN_CHUNK, SCAN_CHUNK), 0)
    s_idx = lax.broadcasted_iota(jnp.int32, (SCAN_CHUNK, SCAN_CHUNK), 1)
    streams = []
    for seq in range(qf_ref.shape[0]):
        streams.append((seq, False, (qf_ref, kf_ref, vf_ref, bf_ref), of_ref, s_idx <= t_idx))
        streams.append((seq, True, (qb_ref, kb_ref, vb_ref, bb_ref), ob_ref, s_idx >= t_idx))

    chunks = qf_ref.shape[1] // SCAN_CHUNK
    items = [(index, h, order) for order in range(chunks) for index in range(len(streams)) for h in range(HEADS)]

    def chunk_rows(reverse, order):
        c = chunks - 1 - order if reverse else order
        return slice(c * SCAN_CHUNK, (c + 1) * SCAN_CHUNK)

    def stage1(index, h, order):
        seq, reverse, refs, _, _ = streams[index]
        return _scan_scores(*refs, st_ref, seq, chunk_rows(reverse, order), h, index * HEADS + h, reverse)

    def stage2(index, h, order, scores, o_inter, v):
        seq, reverse, _, o_ref, seen = streams[index]
        o_ref[seq, chunk_rows(reverse, order), h * HEAD_DIM:(h + 1) * HEAD_DIM] = (
            o_inter + _dot(jnp.where(seen, scores, 0.0).astype(_BF16), v))

    staged = {}
    for step in range(len(items) + SCAN_LAG):
        if step < len(items):
            staged[step] = stage1(*items[step])
        if step >= SCAN_LAG:
            stage2(*items[step - SCAN_LAG], *staged.pop(step - SCAN_LAG))


def _scan(q, kf, kb, v, bf, bb, *, batch):
    seq = q.shape[0] // batch
    seqs = min(SCAN_SEQS, batch)
    chunks = SCAN_SEQS // seqs
    assert batch % seqs == 0 and seq % (chunks * SCAN_CHUNK) == 0, (batch, seq)
    nblocks = seq // (chunks * SCAN_CHUNK)
    shape3 = (batch, seq, D_MODEL)
    block = (seqs, chunks * SCAN_CHUNK, D_MODEL)
    fwd = pl.BlockSpec(block, lambda i, n: (i, n, 0))
    bwd = pl.BlockSpec(block, lambda i, n: (i, nblocks - 1 - n, 0))
    q, kf, kb, v, bf, bb = (a.reshape(shape3) for a in (q, kf, kb, v, bf, bb))
    out = jax.ShapeDtypeStruct(shape3, _F32)
    o_f, o_b = pl.pallas_call(
        _scan_kernel,
        out_shape=(out, out),
        grid=(batch // seqs, nblocks),
        in_specs=[fwd] * 4 + [bwd] * 4,
        out_specs=(fwd, bwd),
        scratch_shapes=[pltpu.VMEM((2 * seqs * HEADS, HEAD_DIM, HEAD_DIM), _F32)],
        compiler_params=_params("parallel", "arbitrary"),
        name="scan",
    )(q, kf, v, bf, q, kb, v, bb)
    return o_f.reshape(batch * seq, D_MODEL), o_b.reshape(batch * seq, D_MODEL)


def _trunk(x, w):
    batch, seq, _ = x.shape
    assert (batch * seq) % max(FFN_ROWS, FFN_HGRN_ROWS, SGU_ROWS, HGRN_ROWS) == 0 and seq % SGU_CHUNK == 0, x.shape
    x = x.reshape(batch * seq, D_MODEL)
    depth = w["norm_w"].shape[0]
    for layer in range(depth):
        last = layer == depth - 1
        j = layer // 2
        x = _ffn(x, w["norm_w"][layer, 0][None], w["ffn_gate"][layer, 0], w["ffn_up"][layer, 0],
                 w["ffn_down"][layer, 0], w["final_norm"][None], final_norm=False)
        hgrn = None
        if layer % 2 == 0:
            x = _sgu(x, w["norm_w"][layer, 1][None], w["sgu_w_in"][j], w["sgu_ln_g"][j][None],
                     w["sgu_ln_b"][j][None], w["sgu_w_s"][j], w["sgu_b_s"][j], w["sgu_w_out"][j])
        else:
            q, kf, kb, v, gs, bf, bb = _hgrn_pre(x, w["norm_w"][layer, 1][None], w["hgrn_w_in"][j],
                                                 w["hgrn_lb_raw"][0], w["hgrn_lb_raw"][1], layer=layer)
            o_f, o_b = _scan(q, kf, kb, v, bf, bb, batch=batch)
            hgrn = (o_f, o_b, gs, w["hgrn_norm_w"][j][None], w["hgrn_w_out"][j])
        x = _ffn(x, w["norm_w"][layer, 2][None], w["ffn_gate"][layer, 1], w["ffn_up"][layer, 1],
                 w["ffn_down"][layer, 1], w["final_norm"][None], final_norm=last, hgrn=hgrn)
    return x.reshape(batch, seq, D_MODEL)


def kernel(x_prompt, x_sample, norm_w, ffn_gate, ffn_up, ffn_down, sgu_w_in, sgu_ln_g, sgu_ln_b, sgu_w_s,
           sgu_b_s, sgu_w_out, hgrn_w_in, hgrn_lb_raw, hgrn_norm_w, hgrn_w_out, final_norm):
    w = {
        "norm_w": norm_w,
        "ffn_gate": ffn_gate.astype(_BF16),
        "ffn_up": ffn_up.astype(_BF16),
        "ffn_down": ffn_down.astype(_BF16),
        "sgu_w_in": sgu_w_in.astype(_BF16),
        "sgu_ln_g": sgu_ln_g,
        "sgu_ln_b": sgu_ln_b,
        "sgu_w_s": sgu_w_s.astype(_BF16),
        "sgu_b_s": jnp.repeat(jnp.swapaxes(sgu_b_s, 1, 2), SGU_GROUP_DIM, axis=2),
        "sgu_w_out": sgu_w_out.astype(_BF16),
        "hgrn_w_in": hgrn_w_in.astype(_BF16),
        "hgrn_lb_raw": hgrn_lb_raw,
        "hgrn_norm_w": hgrn_norm_w,
        "hgrn_w_out": hgrn_w_out.astype(_BF16),
        "final_norm": final_norm,
    }
    return _trunk(x_prompt, w), _trunk(x_sample, w)
```
